```python
import jax, jax.numpy as jnp
from jax import lax
import numpy as np

D_MODEL = 2048
BATCH = 2
SEQ = 4096
DEPTH = 4

CHUNK = 64
N_MIXERS = 2
N_CONV_LAYERS = (DEPTH + N_MIXERS - 1) // N_MIXERS
N_SSM_LAYERS = DEPTH // N_MIXERS

CONV_KERNEL = 31

SSM_EXPAND = 2
SSM_D_INNER = SSM_EXPAND * D_MODEL
SSM_HEAD_DIM = 64
SSM_N_HEADS = SSM_D_INNER // SSM_HEAD_DIM
SSM_N_GROUPS = 8
SSM_HEADS_PER_GROUP = SSM_N_HEADS // SSM_N_GROUPS
SSM_D_STATE = 128
SSM_CONV_KERNEL = 4
SSM_CONV_DIM = SSM_D_INNER + 2 * SSM_N_GROUPS * SSM_D_STATE
SSM_IN_DIM = SSM_D_INNER + SSM_CONV_DIM + SSM_N_HEADS

FFN_HIDDEN = 5632
FFN_CONV_KERNEL = 3

RMS_EPS = 1e-6
LN_EPS = 1e-5

kernel_name = "hybrid_conformer_mamba2_convffn_trunk"


def rms_norm(x, g, eps=RMS_EPS):
    xf = x.astype(jnp.float32)
    y = xf * lax.rsqrt(jnp.mean(xf * xf, axis=-1, keepdims=True) + eps)
    return (y * g.astype(jnp.float32)).astype(x.dtype)


def layer_norm(x, g, b, eps=LN_EPS):
    xf = x.astype(jnp.float32)
    mu = jnp.mean(xf, axis=-1, keepdims=True)
    xc = xf - mu
    var = jnp.mean(xc * xc, axis=-1, keepdims=True)
    y = xc * lax.rsqrt(var + eps) * g.astype(jnp.float32) + b.astype(jnp.float32)
    return y.astype(x.dtype)


def causal_depthwise_conv(x, w, b):
    k, c = w.shape
    xp = jnp.pad(x, ((0, 0), (k - 1, 0), (0, 0)))
    y = lax.conv_general_dilated(
        xp, w[:, None, :].astype(x.dtype), window_strides=(1,), padding="VALID",
        dimension_numbers=("NWC", "WIO", "NWC"), feature_group_count=c)
    return y + b.astype(x.dtype)


def conformer_conv_module(h, w_in, b_in, w_dw, b_dw, ln_g, ln_b, w_out, b_out):
    u = h @ w_in + b_in
    a, gate = jnp.split(u, 2, axis=-1)
    v = a * jax.nn.sigmoid(gate)
    v = causal_depthwise_conv(v, w_dw, b_dw)
    v = jax.nn.silu(layer_norm(v, ln_g, ln_b))
    return v @ w_out + b_out


def segsum(a):
    q = a.shape[-1]
    a_rep = jnp.broadcast_to(a[..., :, None], a.shape + (q,))
    strict = jnp.tril(jnp.ones((q, q), dtype=bool), -1)
    ss = jnp.cumsum(jnp.where(strict, a_rep, 0.0), axis=-2)
    return jnp.where(jnp.tril(jnp.ones((q, q), dtype=bool)), ss, -jnp.inf)


def ssd_chunked(x, dt, a_neg, bm, cm):
    bsz, seq, _, _ = x.shape
    nc = seq // CHUNK
    g, r, p, n = SSM_N_GROUPS, SSM_HEADS_PER_GROUP, SSM_HEAD_DIM, SSM_D_STATE
    xd = (x * dt[..., None]).reshape(bsz, nc, CHUNK, g, r, p)
    a = jnp.moveaxis((dt * a_neg).reshape(bsz, nc, CHUNK, g, r), 2, -1)
    bc = bm.reshape(bsz, nc, CHUNK, g, n)
    cc = cm.reshape(bsz, nc, CHUNK, g, n)
    a_cs = jnp.cumsum(a, axis=-1)
    decay_in = jnp.exp(segsum(a))
    cb = jnp.einsum("bclgn,bcsgn->bcgls", cc, bc)
    y_diag = jnp.einsum("bcgls,bcgrls,bcsgrp->bclgrp", cb, decay_in, xd)
    decay_to_end = jnp.exp(a_cs[..., -1:] - a_cs)
    states = jnp.einsum("bcsgn,bcgrs,bcsgrp->bcgrpn", bc, decay_to_end, xd)
    chunk_decay = jnp.exp(a_cs[..., -1])

    def step(carry, inp):
        st, dec = inp
        return carry * dec[..., None, None] + st, carry

    init = jnp.zeros((bsz, g, r, p, n), dtype=x.dtype)
    _, prev = lax.scan(step, init, (jnp.moveaxis(states, 1, 0), jnp.moveaxis(chunk_decay, 1, 0)))
    prev = jnp.moveaxis(prev, 0, 1)
    y_off = jnp.einsum("bclgn,bcgrpn,bcgrl->bclgrp", cc, prev, jnp.exp(a_cs))
    return (y_diag + y_off).reshape(bsz, seq, g * r, p)


def gated_group_rms_norm(y, z, g):
    yf = (y * jax.nn.silu(z)).astype(jnp.float32)
    shp = yf.shape
    yg = yf.reshape(shp[:-1] + (SSM_N_GROUPS, shp[-1] // SSM_N_GROUPS))
    yg = yg * lax.rsqrt(jnp.mean(yg * yg, axis=-1, keepdims=True) + RMS_EPS)
    return (yg.reshape(shp) * g.astype(jnp.float32)).astype(z.dtype)


def mamba2_mixer(h, w_in, w_conv, b_conv, dt_bias, a_log, d_skip, norm_g, w_out):
    bsz, seq, _ = h.shape
    zxbcdt = h @ w_in
    z, xbc, dt = jnp.split(zxbcdt, [SSM_D_INNER, SSM_D_INNER + SSM_CONV_DIM], axis=-1)
    xbc = jax.nn.silu(causal_depthwise_conv(xbc, w_conv, b_conv))
    gn = SSM_N_GROUPS * SSM_D_STATE
    xs, bm, cm = jnp.split(xbc, [SSM_D_INNER, SSM_D_INNER + gn], axis=-1)
    xs = xs.reshape(bsz, seq, SSM_N_HEADS, SSM_HEAD_DIM).astype(jnp.float32)
    bm = bm.reshape(bsz, seq, SSM_N_GROUPS, SSM_D_STATE).astype(jnp.float32)
    cm = cm.reshape(bsz, seq, SSM_N_GROUPS, SSM_D_STATE).astype(jnp.float32)
    dt = jax.nn.softplus(dt.astype(jnp.float32) + dt_bias.astype(jnp.float32))
    a_neg = -jnp.exp(a_log.astype(jnp.float32))
    y = ssd_chunked(xs, dt, a_neg, bm, cm)
    y = y + d_skip.astype(jnp.float32)[:, None] * xs
    y = y.reshape(bsz, seq, SSM_D_INNER).astype(h.dtype)
    return gated_group_rms_norm(y, z, norm_g) @ w_out


def conv_ffn(h, w_up, w_dw, b_dw, w_down):
    u = causal_depthwise_conv(h @ w_up, w_dw, b_dw)
    gate, val = jnp.split(u, 2, axis=-1)
    return (jax.nn.silu(gate) * val) @ w_down


def setup_inputs(seed: int = 0) -> dict:
    key = jax.random.key(seed)
    ks = jax.random.split(key, 32)
    d, f = D_MODEL, FFN_HIDDEN
    nc, ns = N_CONV_LAYERS, N_SSM_LAYERS

    def nrm(k, shape, scale):
        return jax.random.normal(k, shape, dtype=jnp.float32) * scale

    dt0 = jnp.exp(jax.random.uniform(ks[16], (ns, SSM_N_HEADS), minval=math_log(1e-3), maxval=math_log(1e-1)))
    return {
        "x": nrm(ks[0], (BATCH, SEQ, d), 1.0),
        "norm_mix_g": 1.0 + nrm(ks[1], (DEPTH, d), 0.05),
        "norm_ffn_g": 1.0 + nrm(ks[2], (DEPTH, d), 0.05),
        "norm_final_g": 1.0 + nrm(ks[3], (d,), 0.05),
        "cv_w_in": nrm(ks[4], (nc, d, 2 * d), d ** -0.5),
        "cv_b_in": nrm(ks[5], (nc, 2 * d), 0.02),
        "cv_w_dw": nrm(ks[6], (nc, CONV_KERNEL, d), CONV_KERNEL ** -0.5),
        "cv_b_dw": nrm(ks[7], (nc, d), 0.02),
        "cv_ln_g": 1.0 + nrm(ks[8], (nc, d), 0.05),
        "cv_ln_b": nrm(ks[9], (nc, d), 0.02),
        "cv_w_out": nrm(ks[10], (nc, d, d), d ** -0.5),
        "cv_b_out": nrm(ks[11], (nc, d), 0.02),
        "ssm_w_in": nrm(ks[12], (ns, d, SSM_IN_DIM), d ** -0.5),
        "ssm_w_conv": nrm(ks[13], (ns, SSM_CONV_KERNEL, SSM_CONV_DIM), SSM_CONV_KERNEL ** -0.5),
        "ssm_b_conv": nrm(ks[14], (ns, SSM_CONV_DIM), 0.02),
        "ssm_dt_bias": dt0 + jnp.log(-jnp.expm1(-dt0)),
        "ssm_a_log": jnp.log(jax.random.uniform(ks[17], (ns, SSM_N_HEADS), minval=1.0, maxval=16.0)),
        "ssm_d": 1.0 + nrm(ks[18], (ns, SSM_N_HEADS), 0.1),
        "ssm_norm_g": 1.0 + nrm(ks[19], (ns, SSM_D_INNER), 0.05),
        "ssm_w_out": nrm(ks[20], (ns, SSM_D_INNER, d), SSM_D_INNER ** -0.5),
        "ffn_w_up": nrm(ks[21], (DEPTH, d, 2 * f), d ** -0.5),
        "ffn_w_dw": nrm(ks[22], (DEPTH, FFN_CONV_KERNEL, 2 * f), FFN_CONV_KERNEL ** -0.5),
        "ffn_b_dw": nrm(ks[23], (DEPTH, 2 * f), 0.02),
        "ffn_w_down": nrm(ks[24], (DEPTH, f, d), f ** -0.5),
    }


def math_log(v):
    return float(np.log(v))


def reference(x, norm_mix_g, norm_ffn_g, norm_final_g,
              cv_w_in, cv_b_in, cv_w_dw, cv_b_dw, cv_ln_g, cv_ln_b, cv_w_out, cv_b_out,
              ssm_w_in, ssm_w_conv, ssm_b_conv, ssm_dt_bias, ssm_a_log, ssm_d, ssm_norm_g, ssm_w_out,
              ffn_w_up, ffn_w_dw, ffn_b_dw, ffn_w_down):
    for i in range(DEPTH):
        h = rms_norm(x, norm_mix_g[i])
        j = i // N_MIXERS
        if i % N_MIXERS == 0:
            x = x + conformer_conv_module(h, cv_w_in[j], cv_b_in[j], cv_w_dw[j], cv_b_dw[j],
                                          cv_ln_g[j], cv_ln_b[j], cv_w_out[j], cv_b_out[j])
        else:
            x = x + mamba2_mixer(h, ssm_w_in[j], ssm_w_conv[j], ssm_b_conv[j], ssm_dt_bias[j],
                                 ssm_a_log[j], ssm_d[j], ssm_norm_g[j], ssm_w_out[j])
        x = x + conv_ffn(rms_norm(x, norm_ffn_g[i]), ffn_w_up[i], ffn_w_dw[i], ffn_b_dw[i], ffn_w_down[i])
    return rms_norm(x, norm_final_g)
```

```python
import functools

import jax
import jax.numpy as jnp
from jax import lax
from jax.experimental import pallas as pl
from jax.experimental.pallas import tpu as pltpu

RMS_EPS = 1e-6
LN_EPS = 1e-5

SSD_CHUNK = 64
SSD_HEAD_DIM = 64
SSD_N_GROUPS = 8
SSD_D_STATE = 128
Q_SHIFT = SSD_CHUNK.bit_length() - 1
HEAD_SHIFT = SSD_HEAD_DIM.bit_length() - 1

LANES = 128
SUBLANES = 8
MXU_DIM = 256
VMEM_LIMIT_BYTES = 56 * 1024 * 1024

CONV_HALO = 32
NEG_BIG = -1e30

F32 = jnp.float32
BF16 = jnp.bfloat16


def _params(semantics):
    return pltpu.CompilerParams(dimension_semantics=semantics, vmem_limit_bytes=VMEM_LIMIT_BYTES)


def _rmsnorm_kernel(x_ref, g_ref, o_ref):
    x = x_ref[...]
    ms = jnp.mean(x * x, axis=-1, keepdims=True)
    o_ref[...] = (x * lax.rsqrt(ms + RMS_EPS) * g_ref[...]).astype(o_ref.dtype)


def _rmsnorm(x, gains, layer, out_dtype, tm):
    t, d = x.shape
    return pl.pallas_call(
        _rmsnorm_kernel,
        out_shape=jax.ShapeDtypeStruct((t, d), out_dtype),
        grid=(t // tm,),
        in_specs=[pl.BlockSpec((tm, d), lambda m: (m, 0)),
                  pl.BlockSpec((None, 1, d), lambda m: (layer, 0, 0))],
        out_specs=pl.BlockSpec((tm, d), lambda m: (m, 0)),
        compiler_params=_params(("parallel",)),
        name="rmsnorm",
    )(x, gains)


def _causal_conv_epilogue(u, ext_ref, carry_ref, w_ref, seq_start):
    tm = u.shape[0]
    kc = w_ref.shape[0]

    @pl.when(seq_start)
    def _():
        ext_ref[0:SUBLANES, :] = jnp.zeros((SUBLANES, u.shape[1]), F32)

    @pl.when(jnp.logical_not(seq_start))
    def _():
        ext_ref[0:SUBLANES, :] = carry_ref[...]

    ext_ref[SUBLANES:SUBLANES + tm, :] = u
    carry_ref[...] = u[tm - SUBLANES:tm, :]
    out = u * w_ref[kc - 1:kc, :]
    for k in range(kc - 1):
        shift = kc - 1 - k
        out = out + ext_ref[pl.ds(SUBLANES - shift, tm), :] * w_ref[k:k + 1, :]
    return out


def _mm_kernel(*refs, mode, nw, tiles_per_seq, n_valid):
    a_ref = refs[0]
    w_refs = refs[1:1 + nw]
    rest = refs[1 + nw:]
    m = pl.program_id(1)

    if mode == "glu":
        ba_ref, bg_ref, o_ref, *wb_refs = rest
    elif mode == "res":
        res_ref, o_ref, *wb_refs = rest
    elif mode == "res_bias":
        res_ref, b_ref, o_ref, *wb_refs = rest
    elif mode == "plain" or mode == "mask_cols":
        o_ref, *wb_refs = rest
    elif mode == "conv_silu":
        cw_ref, cb_ref, o_ref, wb0, ext_ref, carry_ref = rest
        wb_refs = [wb0]
    elif mode == "ffn":
        cwg_ref, cwv_ref, cbg_ref, cbv_ref, o_ref, wb0, wb1, extg_ref, extv_ref, carg_ref, carv_ref = rest
        wb_refs = [wb0, wb1]
    else:
        raise ValueError(mode)

    @pl.when(m == 0)
    def _():
        for w_ref, wb_ref in zip(w_refs, wb_refs):
            wb_ref[...] = w_ref[...].astype(BF16)

    a = a_ref[...]
    accs = [jnp.dot(a, wb_ref[...], preferred_element_type=F32) for wb_ref in wb_refs]
    seq_start = (m % tiles_per_seq) == 0

    if mode == "glu":
        o_ref[...] = (accs[0] + ba_ref[...]) * jax.nn.sigmoid(accs[1] + bg_ref[...])
    elif mode == "res":
        o_ref[...] = res_ref[...] + accs[0]
    elif mode == "res_bias":
        o_ref[...] = res_ref[...] + (accs[0] + b_ref[...])
    elif mode == "plain":
        o_ref[...] = accs[0].astype(o_ref.dtype)
    elif mode == "mask_cols":
        lane = lax.broadcasted_iota(jnp.int32, accs[0].shape, 1)
        o_ref[...] = jnp.where(lane < n_valid, accs[0], 0.0)
    elif mode == "conv_silu":
        c = _causal_conv_epilogue(accs[0], ext_ref, carry_ref, cw_ref, seq_start) + cb_ref[...]
        o_ref[...] = c * jax.nn.sigmoid(c)
    elif mode == "ffn":
        gate = _causal_conv_epilogue(accs[0], extg_ref, carg_ref, cwg_ref, seq_start) + cbg_ref[...]
        val = _causal_conv_epilogue(accs[1], extv_ref, carv_ref, cwv_ref, seq_start) + cbv_ref[...]
        o_ref[...] = (gate * jax.nn.sigmoid(gate) * val).astype(o_ref.dtype)


def _mm(a, w, layer, *, mode, col_offs, n_cols, tn, tm, seq_len, out_dtype=F32,
        bias=None, conv_w=None, conv_b=None, res=None, name):
    t, k = a.shape
    nw = len(col_offs)
    assert t % tm == 0 and seq_len % tm == 0 and all(off % tn == 0 for off in col_offs)
    n_tiles = pl.cdiv(n_cols, tn)
    out_cols = n_tiles * tn
    offs = [off // tn for off in col_offs]

    def col_spec(rows, ob):
        return pl.BlockSpec((None, rows, tn), lambda n, m: (layer, 0, n + ob))

    in_specs = [pl.BlockSpec((tm, k), lambda n, m: (m, 0))]
    args = [a]
    for ob in offs:
        in_specs.append(col_spec(k, ob))
        args.append(w)
    tile_spec = pl.BlockSpec((tm, tn), lambda n, m: (m, n))
    scratch = [pltpu.VMEM((k, tn), BF16) for _ in offs]

    if mode == "glu":
        in_specs += [col_spec(1, ob) for ob in offs]
        args += [bias, bias]
    elif mode == "res":
        in_specs.append(tile_spec)
        args.append(res)
    elif mode == "res_bias":
        in_specs += [tile_spec, col_spec(1, offs[0])]
        args += [res, bias]
    elif mode == "conv_silu":
        kc = conv_w.shape[1]
        in_specs += [col_spec(kc, 0), col_spec(1, 0)]
        args += [conv_w, conv_b]
        scratch += [pltpu.VMEM((SUBLANES + tm, tn), F32), pltpu.VMEM((SUBLANES, tn), F32)]
    elif mode == "ffn":
        kc = conv_w.shape[1]
        in_specs += [col_spec(kc, ob) for ob in offs] + [col_spec(1, ob) for ob in offs]
        args += [conv_w, conv_w, conv_b, conv_b]
        scratch += [pltpu.VMEM((SUBLANES + tm, tn), F32) for _ in offs]
        scratch += [pltpu.VMEM((SUBLANES, tn), F32) for _ in offs]

    kern = functools.partial(_mm_kernel, mode=mode, nw=nw, tiles_per_seq=seq_len // tm, n_valid=n_cols)
    return pl.pallas_call(
        kern,
        out_shape=jax.ShapeDtypeStruct((t, out_cols), out_dtype),
        grid=(n_tiles, t // tm),
        in_specs=in_specs,
        out_specs=tile_spec,
        scratch_shapes=scratch,
        compiler_params=_params(("parallel", "arbitrary")),
        name=name,
    )(*args)


def _conf_mid_kernel(cur_ref, halo_ref, w_ref, b_ref, g_ref, beta_ref, o_ref, ext_ref, conv_ref,
                     *, tm, tiles_per_seq, rc):
    kc = w_ref.shape[0]
    d = cur_ref.shape[1]
    seq_start = (pl.program_id(0) % tiles_per_seq) == 0

    @pl.when(seq_start)
    def _():
        ext_ref[0:CONV_HALO, :] = jnp.zeros((CONV_HALO, d), F32)

    @pl.when(jnp.logical_not(seq_start))
    def _():
        ext_ref[0:CONV_HALO, :] = halo_ref[...]

    ext_ref[CONV_HALO:CONV_HALO + tm, :] = cur_ref[...]

    base = CONV_HALO - (kc - 1)
    span = rc + CONV_HALO - SUBLANES
    for cj in range(d // LANES):
        cols = slice(cj * LANES, (cj + 1) * LANES)

        def row_body(ri, carry, cols=cols):
            r0 = pl.multiple_of(ri * rc, rc)
            e = ext_ref[pl.ds(r0, rc + CONV_HALO), cols]
            acc = jnp.zeros((rc, LANES), F32)
            for r in range(SUBLANES):
                taps = [k for k in range(kc) if (base + k) % SUBLANES == r]
                if not taps:
                    continue
                er = e[r:r + span, :]
                for k in taps:
                    q = (base + k) // SUBLANES
                    if q * SUBLANES + rc <= span:
                        tap = er[q * SUBLANES:q * SUBLANES + rc, :]
                    else:
                        tap = e[base + k:base + k + rc, :]
                    acc = acc + tap * w_ref[k:k + 1, cols]
            conv_ref[pl.ds(r0, rc), cols] = acc + b_ref[:, cols]
            return carry

        lax.fori_loop(0, tm // rc, row_body, 0)

    c = conv_ref[...]
    mu = jnp.mean(c, axis=-1, keepdims=True)
    xc = c - mu
    var = jnp.mean(xc * xc, axis=-1, keepdims=True)
    y = xc * lax.rsqrt(var + LN_EPS) * g_ref[...] + beta_ref[...]
    o_ref[...] = (y * jax.nn.sigmoid(y)).astype(o_ref.dtype)


def _conf_mid(v, w_dw, b_dw, ln_g, ln_b, layer, seq_len, tm):
    t, d = v.shape
    kc = w_dw.shape[1]
    assert kc - 1 <= CONV_HALO and seq_len % tm == 0 and tm % CONV_HALO == 0 and d % LANES == 0
    rc = min(64, tm)
    halo_per_tile = tm // CONV_HALO
    row = lambda m: (layer, 0, 0)
    kern = functools.partial(_conf_mid_kernel, tm=tm, tiles_per_seq=seq_len // tm, rc=rc)
    return pl.pallas_call(
        kern,
        out_shape=jax.ShapeDtypeStruct((t, d), BF16),
        grid=(t // tm,),
        in_specs=[pl.BlockSpec((tm, d), lambda m: (m, 0)),
                  pl.BlockSpec((CONV_HALO, d), lambda m: (jnp.maximum(m * halo_per_tile - 1, 0), 0)),
                  pl.BlockSpec((None, kc, d), row),
                  pl.BlockSpec((None, 1, d), row),
                  pl.BlockSpec((None, 1, d), row),
                  pl.BlockSpec((None, 1, d), row)],
        out_specs=pl.BlockSpec((tm, d), lambda m: (m, 0)),
        scratch_shapes=[pltpu.VMEM((CONV_HALO + tm, d), F32), pltpu.VMEM((tm, d), F32)],
        compiler_params=_params(("parallel",)),
        name="conformer_conv_ln_swish",
    )(v, v, w_dw, b_dw, ln_g, ln_b)


def _split3(x):
    hi = x.astype(BF16)
    r1 = x - hi.astype(F32)
    mid = r1.astype(BF16)
    lo = (r1 - mid.astype(F32)).astype(BF16)
    return hi, mid, lo


def _ssd_kernel(x_ref, b_ref, c_ref, z_ref, dt_ref, dtb_ref, alog_ref, dskip_ref, ng_ref, o_ref,
                state_ref, dtc_ref, acs_ref, dte_ref, acse_ref, *, tl, gw):
    q = SSD_CHUNK
    lt = pl.program_id(1)
    g = pl.program_id(2)
    hpg = gw // SSD_HEAD_DIM

    @pl.when(lt == 0)
    def _():
        state_ref[g] = jnp.zeros(state_ref.shape[1:], F32)

    @pl.when(g == 0)
    def _():
        pre = dt_ref[...] + dtb_ref[...]
        dt = jnp.maximum(pre, 0.0) + jnp.log1p(jnp.exp(-jnp.abs(pre)))
        a = dt * (-jnp.exp(alog_ref[...]))
        r = lax.broadcasted_iota(jnp.int32, (tl, tl), 0)
        c = lax.broadcasted_iota(jnp.int32, (tl, tl), 1)
        tri = jnp.where((r >= c) & ((r >> Q_SHIFT) == (c >> Q_SHIFT)), 1.0, 0.0).astype(BF16)
        acs = jnp.zeros((tl, LANES), F32)
        for part in _split3(a):
            acs = acs + jnp.dot(tri, part, preferred_element_type=F32)
        dtc_ref[...] = dt
        acs_ref[...] = acs

    hrow = lax.broadcasted_iota(jnp.int32, (LANES, gw), 0)
    hcol = lax.broadcasted_iota(jnp.int32, (LANES, gw), 1)
    expand = jnp.where(hrow == g * hpg + (hcol >> HEAD_SHIFT), 1.0, 0.0).astype(BF16)
    dte = jnp.zeros((tl, gw), F32)
    for part in _split3(dtc_ref[...]):
        dte = dte + jnp.dot(part, expand, preferred_element_type=F32)
    acse = jnp.zeros((tl, gw), F32)
    for part in _split3(acs_ref[...]):
        acse = acse + jnp.dot(part, expand, preferred_element_type=F32)
    dte_ref[...] = dte
    acse_ref[...] = acse

    row = lax.broadcasted_iota(jnp.int32, (q, gw), 0)
    pos = lax.broadcasted_iota(jnp.int32, (q, gw), 1) & (SSD_HEAD_DIM - 1)
    causal = row >= pos
    diagonal = row == pos
    br = lax.broadcasted_iota(jnp.int32, (MXU_DIM, MXU_DIM), 0) >> HEAD_SHIFT
    bc = lax.broadcasted_iota(jnp.int32, (MXU_DIM, MXU_DIM), 1) >> HEAD_SHIFT
    same_head = br == bc
    heads_per_dot = MXU_DIM // SSD_HEAD_DIM
    dskip = dskip_ref[...]
    ng = ng_ref[...]

    def chunk(ci, carry):
        r0 = pl.multiple_of(ci * q, q)
        x = x_ref[pl.ds(r0, q), :]
        bm = b_ref[pl.ds(r0, q), :].astype(BF16)
        cm = c_ref[pl.ds(r0, q), :].astype(BF16)
        acs = acse_ref[pl.ds(r0, q), :]
        a_last = acse_ref[pl.ds(r0 + q - 1, 1), :]
        xd = x * dte_ref[pl.ds(r0, q), :]
        xd_bf = xd.astype(BF16)

        prev = state_ref[g]
        y = jnp.dot(cm, prev.astype(BF16), preferred_element_type=F32) * jnp.exp(acs)

        wgt = (jnp.exp(a_last - acs) * xd).astype(BF16)
        s_new = lax.dot_general(bm, wgt, (((0,), (0,)), ((), ())), preferred_element_type=F32)
        state_ref[g] = prev * jnp.exp(a_last) + s_new

        cb = lax.dot_general(cm, jnp.concatenate([bm] * hpg, axis=0), (((1,), (1,)), ((), ())),
                             preferred_element_type=F32)
        acs_s = jnp.sum(jnp.where(diagonal, acs, 0.0), axis=0, keepdims=True)
        decay = jnp.exp(jnp.where(causal, acs - acs_s, NEG_BIG))
        mix = (cb * decay).astype(BF16)
        parts = []
        for j in range(gw // MXU_DIM):
            cols = slice(j * MXU_DIM, (j + 1) * MXU_DIM)
            blockdiag = jnp.where(same_head, jnp.concatenate([xd_bf[:, cols]] * heads_per_dot, axis=0),
                                  jnp.zeros((), BF16))
            parts.append(jnp.dot(mix[:, cols], blockdiag, preferred_element_type=F32))
        y = y + jnp.concatenate(parts, axis=1) + dskip * x

        z = z_ref[pl.ds(r0, q), :]
        yg = y * (z * jax.nn.sigmoid(z))
        ms = jnp.mean(yg * yg, axis=-1, keepdims=True)
        o_ref[pl.ds(r0, q), :] = (yg * lax.rsqrt(ms + RMS_EPS) * ng).astype(o_ref.dtype)
        return carry

    lax.fori_loop(0, tl // q, chunk, 0)


def _ssd(xbc, z, dt, dt_bias, a_log, d_skip, norm_g, layer, batch, seq_len, tl):
    t, d_inner = z.shape
    gw = d_inner // SSD_N_GROUPS
    assert gw % MXU_DIM == 0 and SSD_D_STATE == LANES and seq_len % tl == 0 and tl % SSD_CHUNK == 0
    nl = seq_len // tl
    b_blk = d_inner // SSD_D_STATE
    c_blk = b_blk + SSD_N_GROUPS
    rows = lambda b, l, g: (b * nl + l, g)
    head_row = lambda b, l, g: (layer, 0, 0)
    group_row = lambda b, l, g: (layer, 0, g)
    kern = functools.partial(_ssd_kernel, tl=tl, gw=gw)
    return pl.pallas_call(
        kern,
        out_shape=jax.ShapeDtypeStruct((t, d_inner), BF16),
        grid=(batch, nl, SSD_N_GROUPS),
        in_specs=[pl.BlockSpec((tl, gw), rows),
                  pl.BlockSpec((tl, SSD_D_STATE), lambda b, l, g: (b * nl + l, b_blk + g)),
                  pl.BlockSpec((tl, SSD_D_STATE), lambda b, l, g: (b * nl + l, c_blk + g)),
                  pl.BlockSpec((tl, gw), rows),
                  pl.BlockSpec((tl, LANES), lambda b, l, g: (b * nl + l, 0)),
                  pl.BlockSpec((None, 1, LANES), head_row),
                  pl.BlockSpec((None, 1, LANES), head_row),
                  pl.BlockSpec((None, 1, gw), group_row),
                  pl.BlockSpec((None, 1, gw), group_row)],
        out_specs=pl.BlockSpec((tl, gw), rows),
        scratch_shapes=[pltpu.VMEM((SSD_N_GROUPS, SSD_D_STATE, gw), F32),
                        pltpu.VMEM((tl, LANES), F32),
                        pltpu.VMEM((tl, LANES), F32),
                        pltpu.VMEM((tl, gw), F32),
                        pltpu.VMEM((tl, gw), F32)],
        compiler_params=_params(("parallel", "arbitrary", "arbitrary")),
        name="ssd_scan_gate_norm",
    )(xbc, xbc, xbc, z, dt, dt_bias, a_log, d_skip, norm_g)


def _row3(p):
    return p.reshape(p.shape[0], 1, p.shape[1])


def _pad_lanes(p):
    return jnp.pad(p, ((0, 0), (0, LANES - p.shape[1])))


def kernel(x, norm_mix_g, norm_ffn_g, norm_final_g, cv_w_in, cv_b_in, cv_w_dw, cv_b_dw, cv_ln_g, cv_ln_b, cv_w_out, cv_b_out, ssm_w_in, ssm_w_conv, ssm_b_conv, ssm_dt_bias, ssm_a_log, ssm_d, ssm_norm_g, ssm_w_out, ffn_w_up, ffn_w_dw, ffn_b_dw, ffn_w_down):
    batch, seq_len, d = x.shape
    t = batch * seq_len
    depth = norm_mix_g.shape[0]
    d_inner = ssm_w_out.shape[1]
    n_heads = ssm_dt_bias.shape[1]
    gn = SSD_N_GROUPS * SSD_D_STATE
    f = ffn_w_down.shape[1]
    assert n_heads * SSD_HEAD_DIM == d_inner and n_heads <= LANES
    assert ssm_w_in.shape[2] == 2 * d_inner + 2 * gn + n_heads

    tm = min(1024, seq_len)
    tm_k = min(512, seq_len)
    tn = min(512, d)
    tnorm = min(512, seq_len)
    tconv = min(256, seq_len)
    tl = min(256, seq_len)

    norm_mix_g, norm_ffn_g = _row3(norm_mix_g), _row3(norm_ffn_g)
    norm_final_g = norm_final_g.reshape(1, 1, d)
    cv_b_in, cv_b_dw, cv_ln_g, cv_ln_b, cv_b_out = map(_row3, (cv_b_in, cv_b_dw, cv_ln_g, cv_ln_b, cv_b_out))
    ssm_b_conv, ssm_norm_g, ffn_b_dw = map(_row3, (ssm_b_conv, ssm_norm_g, ffn_b_dw))
    dt_bias = _row3(_pad_lanes(ssm_dt_bias))
    a_log = _row3(_pad_lanes(ssm_a_log))
    d_skip = _row3(jnp.repeat(ssm_d, SSD_HEAD_DIM, axis=1))

    xf = x.reshape(t, d)
    for i in range(depth):
        j = i // 2
        h = _rmsnorm(xf, norm_mix_g, i, BF16, tnorm)
        if i % 2 == 0:
            v = _mm(h, cv_w_in, j, mode="glu", col_offs=(0, d), n_cols=d, tn=tn, tm=tm, seq_len=seq_len,
                    bias=cv_b_in, name="conformer_in_glu")
            s = _conf_mid(v, cv_w_dw, cv_b_dw, cv_ln_g, cv_ln_b, j, seq_len, tconv)
            xf = _mm(s, cv_w_out, j, mode="res_bias", col_offs=(0,), n_cols=d, tn=tn, tm=tm, seq_len=seq_len,
                     res=xf, bias=cv_b_out, name="conformer_out")
        else:
            z = _mm(h, ssm_w_in, j, mode="plain", col_offs=(0,), n_cols=d_inner, tn=tn, tm=tm,
                    seq_len=seq_len, name="ssm_in_z")
            xbc = _mm(h, ssm_w_in, j, mode="conv_silu", col_offs=(d_inner,), n_cols=d_inner + 2 * gn, tn=tn,
                      tm=tm, seq_len=seq_len, conv_w=ssm_w_conv, conv_b=ssm_b_conv, name="ssm_in_xbc")
            dt = _mm(h, ssm_w_in, j, mode="mask_cols", col_offs=(2 * d_inner + 2 * gn,), n_cols=n_heads,
                     tn=LANES, tm=tm, seq_len=seq_len, name="ssm_in_dt")
            yn = _ssd(xbc, z, dt, dt_bias, a_log, d_skip, ssm_norm_g, j, batch, seq_len, tl)
            xf = _mm(yn, ssm_w_out, j, mode="res", col_offs=(0,), n_cols=d, tn=tn, tm=tm_k, seq_len=seq_len,
                     res=xf, name="ssm_out")
        h = _rmsnorm(xf, norm_ffn_g, i, BF16, tnorm)
        act = _mm(h, ffn_w_up, i, mode="ffn", col_offs=(0, f), n_cols=f, tn=tn, tm=tm, seq_len=seq_len,
                  out_dtype=BF16, conv_w=ffn_w_dw, conv_b=ffn_b_dw, name="ffn_up_conv_gate")
        xf = _mm(act, ffn_w_down, i, mode="res", col_offs=(0,), n_cols=d, tn=tn, tm=tm_k, seq_len=seq_len,
                 res=xf, name="ffn_down")
    out = _rmsnorm(xf, norm_final_g, 0, F32, tnorm)
    return out.reshape(batch, seq_len, d)
```

```python
import functools

import jax
import jax.numpy as jnp
from jax import lax
from jax.experimental import pallas as pl
from jax.experimental.pallas import tpu as pltpu

RMS_EPS = 1e-6
LN_EPS = 1e-5

SSD_CHUNK = 64
SSD_HEAD_DIM = 64
SSD_N_GROUPS = 8
SSD_D_STATE = 128
Q_SHIFT = SSD_CHUNK.bit_length() - 1
HEAD_SHIFT = SSD_HEAD_DIM.bit_length() - 1

LANES = 128
SUBLANES = 8
MXU_DIM = 256
VMEM_LIMIT_BYTES = 56 * 1024 * 1024

CONV_HALO = 32
NEG_BIG = -1e30

F32 = jnp.float32
BF16 = jnp.bfloat16


def _params(semantics):
    return pltpu.CompilerParams(dimension_semantics=semantics, vmem_limit_bytes=VMEM_LIMIT_BYTES)


def _rmsnorm_kernel(x_ref, g_ref, o_ref):
    x = x_ref[...]
    ms = jnp.mean(x * x, axis=-1, keepdims=True)
    o_ref[...] = (x * lax.rsqrt(ms + RMS_EPS) * g_ref[...]).astype(o_ref.dtype)


def _rmsnorm(x, gains, layer, out_dtype, tm):
    t, d = x.shape
    return pl.pallas_call(
        _rmsnorm_kernel,
        out_shape=jax.ShapeDtypeStruct((t, d), out_dtype),
        grid=(t // tm,),
        in_specs=[pl.BlockSpec((tm, d), lambda m: (m, 0)),
                  pl.BlockSpec((None, 1, d), lambda m: (layer, 0, 0))],
        out_specs=pl.BlockSpec((tm, d), lambda m: (m, 0)),
        compiler_params=_params(("parallel",)),
        name="rmsnorm",
    )(x, gains)


def _tile_rows(x8, rows):
    return jnp.tile(x8, (rows // SUBLANES, 1))


def _causal_conv_rows(u, tail, p8_ref, kc):
    rc = u.shape[0]
    ext = jnp.concatenate([tail, u], axis=0)
    out = u * _tile_rows(p8_ref[kc - 1], rc) + _tile_rows(p8_ref[kc], rc)
    for k in range(kc - 1):
        start = SUBLANES - (kc - 1 - k)
        out = out + ext[start:start + rc, :] * _tile_rows(p8_ref[k], rc)
    return out


def _mm_kernel(*refs, mode, nw, tiles_per_seq, n_valid, rc, transposed):
    a_ref = refs[0]
    w_refs = refs[1:1 + nw]
    rest = refs[1 + nw:]
    m = pl.program_id(1)
    tm = a_ref.shape[0]
    conv_refs, p8_refs, carry_ref = (), (), None

    if mode == "glu":
        ba_ref, bg_ref, o_ref, *wb_refs = rest
    elif mode == "res":
        res_ref, o_ref, *wb_refs = rest
    elif mode == "res_bias":
        res_ref, b_ref, o_ref, *wb_refs = rest
    elif mode == "plain" or mode == "mask_cols":
        o_ref, *wb_refs = rest
    elif mode == "conv_silu":
        cw_ref, cb_ref, o_ref, wb0, p8, carry_ref = rest
        wb_refs, conv_refs, p8_refs = [wb0], [(cw_ref, cb_ref)], [p8]
    elif mode == "ffn":
        cwg_ref, cwv_ref, cbg_ref, cbv_ref, o_ref, wb0, wb1, p8g, p8v, carry_ref = rest
        wb_refs, conv_refs, p8_refs = [wb0, wb1], [(cwg_ref, cbg_ref), (cwv_ref, cbv_ref)], [p8g, p8v]
    else:
        raise ValueError(mode)
    kc = conv_refs[0][0].shape[0] if conv_refs else 0

    @pl.when(m == 0)
    def _():
        for w_ref, wb_ref in zip(w_refs, wb_refs):
            wb_ref[...] = w_ref[...].astype(BF16)
        for (cw_ref, cb_ref), p8_ref in zip(conv_refs, p8_refs):
            for k in range(kc):
                p8_ref[k] = jnp.broadcast_to(cw_ref[k:k + 1, :], p8_ref.shape[1:])
            p8_ref[kc] = jnp.broadcast_to(cb_ref[...], p8_ref.shape[1:])

    if conv_refs:
        @pl.when((m % tiles_per_seq) == 0)
        def _():
            carry_ref[...] = jnp.zeros(carry_ref.shape, F32)

    def matmul(a, wb_ref):
        dims = (((1,), (1,)), ((), ())) if transposed else (((1,), (0,)), ((), ()))
        return lax.dot_general(a, wb_ref[...], dims, preferred_element_type=F32)

    def chunk_matmuls(c):
        a = a_ref[c * rc:(c + 1) * rc, :]
        return [matmul(a, wb_ref) for wb_ref in wb_refs]

    tails = [carry_ref[i] for i in range(len(conv_refs))]
    n_chunks = tm // rc
    accs_next = chunk_matmuls(0)
    for c in range(n_chunks):
        rows = slice(c * rc, (c + 1) * rc)
        accs = accs_next
        if c + 1 < n_chunks:
            accs_next = chunk_matmuls(c + 1)
        if mode == "glu":
            o_ref[rows, :] = (accs[0] + ba_ref[...]) * jax.nn.sigmoid(accs[1] + bg_ref[...])
        elif mode == "res":
            o_ref[rows, :] = res_ref[rows, :] + accs[0]
        elif mode == "res_bias":
            o_ref[rows, :] = res_ref[rows, :] + (accs[0] + b_ref[...])
        elif mode == "plain":
            o_ref[rows, :] = accs[0].astype(o_ref.dtype)
        elif mode == "mask_cols":
            lane = lax.broadcasted_iota(jnp.int32, accs[0].shape, 1)
            o_ref[rows, :] = jnp.where(lane < n_valid, accs[0], 0.0)
        else:
            convs = [_causal_conv_rows(u, tail, p8_ref, kc) for u, tail, p8_ref in zip(accs, tails, p8_refs)]
            tails = [u[rc - SUBLANES:rc, :] for u in accs]
            if mode == "conv_silu":
                o_ref[rows, :] = convs[0] * jax.nn.sigmoid(convs[0])
            else:
                o_ref[rows, :] = (convs[0] * jax.nn.sigmoid(convs[0]) * convs[1]).astype(o_ref.dtype)
    for i, tail in enumerate(tails):
        carry_ref[i] = tail


def _mm(a, w, layer, *, mode, col_offs, n_cols, tn, tm, rc, seq_len, out_dtype=F32, transposed=False,
        bias=None, conv_w=None, conv_b=None, res=None, name):
    t, k = a.shape
    nw = len(col_offs)
    assert t % tm == 0 and seq_len % tm == 0 and tm % rc == 0 and all(off % tn == 0 for off in col_offs)
    n_tiles = pl.cdiv(n_cols, tn)
    out_cols = n_tiles * tn
    offs = [off // tn for off in col_offs]

    def col_spec(rows, ob):
        return pl.BlockSpec((None, rows, tn), lambda n, m: (layer, 0, n + ob))

    def weight_spec(ob):
        if transposed:
            return pl.BlockSpec((None, tn, k), lambda n, m: (layer, n + ob, 0))
        return col_spec(k, ob)

    in_specs = [pl.BlockSpec((tm, k), lambda n, m: (m, 0))]
    args = [a]
    for ob in offs:
        in_specs.append(weight_spec(ob))
        args.append(w)
    tile_spec = pl.BlockSpec((tm, tn), lambda n, m: (m, n))
    scratch = [pltpu.VMEM((tn, k) if transposed else (k, tn), BF16) for _ in offs]

    if mode == "glu":
        in_specs += [col_spec(1, ob) for ob in offs]
        args += [bias, bias]
    elif mode == "res":
        in_specs.append(tile_spec)
        args.append(res)
    elif mode == "res_bias":
        in_specs += [tile_spec, col_spec(1, offs[0])]
        args += [res, bias]
    elif mode == "conv_silu":
        kc = conv_w.shape[1]
        in_specs += [col_spec(kc, 0), col_spec(1, 0)]
        args += [conv_w, conv_b]
    elif mode == "ffn":
        kc = conv_w.shape[1]
        in_specs += [col_spec(kc, ob) for ob in offs] + [col_spec(1, ob) for ob in offs]
        args += [conv_w, conv_w, conv_b, conv_b]
    if mode in ("conv_silu", "ffn"):
        assert kc - 1 <= SUBLANES
        scratch += [pltpu.VMEM((kc + 1, SUBLANES, tn), F32) for _ in offs]
        scratch += [pltpu.VMEM((nw, SUBLANES, tn), F32)]

    kern = functools.partial(_mm_kernel, mode=mode, nw=nw, tiles_per_seq=seq_len // tm, n_valid=n_cols,
                             rc=rc, transposed=transposed)
    return pl.pallas_call(
        kern,
        out_shape=jax.ShapeDtypeStruct((t, out_cols), out_dtype),
        grid=(n_tiles, t // tm),
        in_specs=in_specs,
        out_specs=tile_spec,
        scratch_shapes=scratch,
        compiler_params=_params(("parallel", "arbitrary")),
        name=name,
    )(*args)


def _conf_mid_kernel(cur_ref, halo_ref, w_ref, b_ref, g_ref, beta_ref, o_ref, ext_ref, conv_ref, w8_ref,
                     *, tm, tiles_per_seq, rc):
    kc = w_ref.shape[0]
    d = cur_ref.shape[1]
    seq_start = (pl.program_id(0) % tiles_per_seq) == 0

    @pl.when(seq_start)
    def _():
        ext_ref[0:CONV_HALO, :] = jnp.zeros((CONV_HALO, d), F32)

    @pl.when(jnp.logical_not(seq_start))
    def _():
        ext_ref[0:CONV_HALO, :] = halo_ref[...]

    ext_ref[CONV_HALO:CONV_HALO + tm, :] = cur_ref[...]
    for k in range(kc):
        w8_ref[k] = jnp.broadcast_to(w_ref[k:k + 1, :], (SUBLANES, d))

    base = CONV_HALO - (kc - 1)
    for cj in range(d // LANES):
        cols = slice(cj * LANES, (cj + 1) * LANES)

        def row_body(ri, carry, cols=cols):
            r0 = pl.multiple_of(ri * rc, rc)
            e = ext_ref[pl.ds(r0, rc + CONV_HALO), cols]
            acc = jnp.zeros((rc, LANES), F32) + b_ref[:, cols]
            for r in range(SUBLANES):
                taps = [k for k in range(kc) if (base + k) % SUBLANES == r]
                if not taps:
                    continue
                n = rc if r == 0 else rc + SUBLANES
                part = None
                for k in taps:
                    lo = (base + k) // SUBLANES * SUBLANES
                    term = e[lo:lo + n, :] * _tile_rows(w8_ref[k, :, cols], n)
                    part = term if part is None else part + term
                acc = acc + part[r:r + rc, :]
            conv_ref[pl.ds(r0, rc), cols] = acc
            return carry

        lax.fori_loop(0, tm // rc, row_body, 0)

    c = conv_ref[...]
    mu = jnp.mean(c, axis=-1, keepdims=True)
    xc = c - mu
    var = jnp.mean(xc * xc, axis=-1, keepdims=True)
    y = xc * lax.rsqrt(var + LN_EPS) * g_ref[...] + beta_ref[...]
    o_ref[...] = (y * jax.nn.sigmoid(y)).astype(o_ref.dtype)


def _conf_mid(v, w_dw, b_dw, ln_g, ln_b, layer, seq_len, tm):
    t, d = v.shape
    kc = w_dw.shape[1]
    assert kc - 1 <= CONV_HALO and seq_len % tm == 0 and tm % CONV_HALO == 0 and d % LANES == 0
    rc = min(64, tm)
    halo_per_tile = tm // CONV_HALO
    row = lambda m: (layer, 0, 0)
    kern = functools.partial(_conf_mid_kernel, tm=tm, tiles_per_seq=seq_len // tm, rc=rc)
    return pl.pallas_call(
        kern,
        out_shape=jax.ShapeDtypeStruct((t, d), BF16),
        grid=(t // tm,),
        in_specs=[pl.BlockSpec((tm, d), lambda m: (m, 0)),
                  pl.BlockSpec((CONV_HALO, d), lambda m: (jnp.maximum(m * halo_per_tile - 1, 0), 0)),
                  pl.BlockSpec((None, kc, d), row),
                  pl.BlockSpec((None, 1, d), row),
                  pl.BlockSpec((None, 1, d), row),
                  pl.BlockSpec((None, 1, d), row)],
        out_specs=pl.BlockSpec((tm, d), lambda m: (m, 0)),
        scratch_shapes=[pltpu.VMEM((CONV_HALO + tm, d), F32), pltpu.VMEM((tm, d), F32),
                        pltpu.VMEM((kc, SUBLANES, d), F32)],
        compiler_params=_params(("parallel",)),
        name="conformer_conv_ln_swish",
    )(v, v, w_dw, b_dw, ln_g, ln_b)


def _split2(x):
    hi = x.astype(BF16)
    lo = (x - hi.astype(F32)).astype(BF16)
    return hi, lo


def _split3(x):
    hi, lo = _split2(x)
    lo2 = (x - hi.astype(F32) - lo.astype(F32)).astype(BF16)
    return hi, lo, lo2


def _ssd_kernel(x_ref, b_ref, c_ref, z_ref, dt_ref, dtb_ref, alog_ref, dskip_ref, ng_ref, o_ref,
                state_ref, parts_ref, dte_ref, acse_ref, *, tl, gw):
    q = SSD_CHUNK
    lt = pl.program_id(1)
    g = pl.program_id(2)
    hpg = gw // SSD_HEAD_DIM

    @pl.when(lt == 0)
    def _():
        state_ref[g] = jnp.zeros(state_ref.shape[1:], F32)

    @pl.when(g == 0)
    def _():
        pre = dt_ref[...] + dtb_ref[...]
        dt = jnp.maximum(pre, 0.0) + jnp.log1p(jnp.exp(-jnp.abs(pre)))
        a = dt * (-jnp.exp(alog_ref[...]))
        r = lax.broadcasted_iota(jnp.int32, (tl, tl), 0)
        c = lax.broadcasted_iota(jnp.int32, (tl, tl), 1)
        tri = jnp.where((r >= c) & ((r >> Q_SHIFT) == (c >> Q_SHIFT)), 1.0, 0.0).astype(BF16)
        acs = jnp.zeros((tl, LANES), F32)
        for part in _split3(a):
            acs = acs + jnp.dot(tri, part, preferred_element_type=F32)
        for i, part in enumerate(_split2(dt) + _split2(acs)):
            parts_ref[i * tl:(i + 1) * tl, :] = part

    hrow = lax.broadcasted_iota(jnp.int32, (LANES, gw), 0)
    hcol = lax.broadcasted_iota(jnp.int32, (LANES, gw), 1)
    expand = jnp.where(hrow == g * hpg + (hcol >> HEAD_SHIFT), 1.0, 0.0).astype(BF16)
    ex = jnp.dot(parts_ref[...], expand, preferred_element_type=F32)
    dte_ref[...] = ex[0:tl] + ex[tl:2 * tl]
    acse_ref[...] = ex[2 * tl:3 * tl] + ex[3 * tl:4 * tl]

    row = lax.broadcasted_iota(jnp.int32, (q, gw), 0)
    pos = lax.broadcasted_iota(jnp.int32, (q, gw), 1) & (SSD_HEAD_DIM - 1)
    causal = row >= pos
    diagonal = row == pos
    br = lax.broadcasted_iota(jnp.int32, (MXU_DIM, MXU_DIM), 0) >> HEAD_SHIFT
    bc = lax.broadcasted_iota(jnp.int32, (MXU_DIM, MXU_DIM), 1) >> HEAD_SHIFT
    same_head = br == bc
    heads_per_dot = MXU_DIM // SSD_HEAD_DIM
    dskip = dskip_ref[...]
    ng = ng_ref[...]

    for ci in range(tl // q):
        r0 = ci * q
        x = x_ref[pl.ds(r0, q), :]
        bm = b_ref[pl.ds(r0, q), :].astype(BF16)
        cm = c_ref[pl.ds(r0, q), :].astype(BF16)
        acs = acse_ref[pl.ds(r0, q), :]
        a_last = acse_ref[pl.ds(r0 + q - 1, 1), :]
        xd = x * dte_ref[pl.ds(r0, q), :]
        xd_bf = xd.astype(BF16)

        prev = state_ref[g]
        y = jnp.dot(cm, prev.astype(BF16), preferred_element_type=F32) * jnp.exp(acs)

        wgt = (jnp.exp(a_last - acs) * xd).astype(BF16)
        s_new = lax.dot_general(bm, wgt, (((0,), (0,)), ((), ())), preferred_element_type=F32)
        state_ref[g] = prev * jnp.exp(a_last) + s_new

        cb = lax.dot_general(cm, jnp.concatenate([bm] * hpg, axis=0), (((1,), (1,)), ((), ())),
                             preferred_element_type=F32)
        acs_s = jnp.sum(jnp.where(diagonal, acs, 0.0), axis=0, keepdims=True)
        decay = jnp.exp(jnp.where(causal, acs - acs_s, NEG_BIG))
        mix = (cb * decay).astype(BF16)
        parts = []
        for j in range(gw // MXU_DIM):
            cols = slice(j * MXU_DIM, (j + 1) * MXU_DIM)
            blockdiag = jnp.where(same_head, jnp.concatenate([xd_bf[:, cols]] * heads_per_dot, axis=0),
                                  jnp.zeros((), BF16))
            parts.append(jnp.dot(mix[:, cols], blockdiag, preferred_element_type=F32))
        y = y + jnp.concatenate(parts, axis=1) + dskip * x

        z = z_ref[pl.ds(r0, q), :]
        yg = y * (z * jax.nn.sigmoid(z))
        ms = jnp.mean(yg * yg, axis=-1, keepdims=True)
        o_ref[pl.ds(r0, q), :] = (yg * lax.rsqrt(ms + RMS_EPS) * ng).astype(o_ref.dtype)


def _ssd(xbc, z, dt, dt_bias, a_log, d_skip, norm_g, layer, batch, seq_len, tl):
    t, d_inner = z.shape
    gw = d_inner // SSD_N_GROUPS
    assert gw % MXU_DIM == 0 and SSD_D_STATE == LANES and seq_len % tl == 0 and tl % SSD_CHUNK == 0
    nl = seq_len // tl
    b_blk = d_inner // SSD_D_STATE
    c_blk = b_blk + SSD_N_GROUPS
    rows = lambda b, l, g: (b * nl + l, g)
    head_row = lambda b, l, g: (layer, 0, 0)
    group_row = lambda b, l, g: (layer, 0, g)
    kern = functools.partial(_ssd_kernel, tl=tl, gw=gw)
    return pl.pallas_call(
        kern,
        out_shape=jax.ShapeDtypeStruct((t, d_inner), BF16),
        grid=(batch, nl, SSD_N_GROUPS),
        in_specs=[pl.BlockSpec((tl, gw), rows),
                  pl.BlockSpec((tl, SSD_D_STATE), lambda b, l, g: (b * nl + l, b_blk + g)),
                  pl.BlockSpec((tl, SSD_D_STATE), lambda b, l, g: (b * nl + l, c_blk + g)),
                  pl.BlockSpec((tl, gw), rows),
                  pl.BlockSpec((tl, LANES), lambda b, l, g: (b * nl + l, 0)),
                  pl.BlockSpec((None, 1, LANES), head_row),
                  pl.BlockSpec((None, 1, LANES), head_row),
                  pl.BlockSpec((None, 1, gw), group_row),
                  pl.BlockSpec((None, 1, gw), group_row)],
        out_specs=pl.BlockSpec((tl, gw), rows),
        scratch_shapes=[pltpu.VMEM((SSD_N_GROUPS, SSD_D_STATE, gw), F32),
                        pltpu.VMEM((4 * tl, LANES), BF16),
                        pltpu.VMEM((tl, gw), F32),
                        pltpu.VMEM((tl, gw), F32)],
        compiler_params=_params(("parallel", "arbitrary", "arbitrary")),
        name="ssd_scan_gate_norm",
    )(xbc, xbc, xbc, z, dt, dt_bias, a_log, d_skip, norm_g)


def _row3(p):
    return p.reshape(p.shape[0], 1, p.shape[1])


def _pad_lanes(p):
    return jnp.pad(p, ((0, 0), (0, LANES - p.shape[1])))


def kernel(x, norm_mix_g, norm_ffn_g, norm_final_g, cv_w_in, cv_b_in, cv_w_dw, cv_b_dw, cv_ln_g, cv_ln_b, cv_w_out, cv_b_out, ssm_w_in, ssm_w_conv, ssm_b_conv, ssm_dt_bias, ssm_a_log, ssm_d, ssm_norm_g, ssm_w_out, ffn_w_up, ffn_w_dw, ffn_b_dw, ffn_w_down):
    batch, seq_len, d = x.shape
    t = batch * seq_len
    depth = norm_mix_g.shape[0]
    d_inner = ssm_w_out.shape[1]
    n_heads = ssm_dt_bias.shape[1]
    gn = SSD_N_GROUPS * SSD_D_STATE
    f = ffn_w_down.shape[1]
    assert n_heads * SSD_HEAD_DIM == d_inner and n_heads <= LANES
    assert ssm_w_in.shape[2] == 2 * d_inner + 2 * gn + n_heads

    tm = min(1024, seq_len)
    tm_k = min(512, seq_len)
    tn = min(512, d)
    tnorm = min(512, seq_len)
    tconv = min(256, seq_len)
    tl = min(256, seq_len)
    rc = min(256, seq_len)
    rc_t = min(512, seq_len)

    norm_mix_g, norm_ffn_g = _row3(norm_mix_g), _row3(norm_ffn_g)
    norm_final_g = norm_final_g.reshape(1, 1, d)
    cv_b_in, cv_b_dw, cv_ln_g, cv_ln_b, cv_b_out = map(_row3, (cv_b_in, cv_b_dw, cv_ln_g, cv_ln_b, cv_b_out))
    ssm_b_conv, ssm_norm_g, ffn_b_dw = map(_row3, (ssm_b_conv, ssm_norm_g, ffn_b_dw))
    dt_bias = _row3(_pad_lanes(ssm_dt_bias))
    a_log = _row3(_pad_lanes(ssm_a_log))
    d_skip = _row3(jnp.repeat(ssm_d, SSD_HEAD_DIM, axis=1))
    ssm_w_in_t = jnp.swapaxes(ssm_w_in, 1, 2)

    xf = x.reshape(t, d)
    for i in range(depth):
        j = i // 2
        h = _rmsnorm(xf, norm_mix_g, i, BF16, tnorm)
        if i % 2 == 0:
            v = _mm(h, cv_w_in, j, mode="glu", col_offs=(0, d), n_cols=d, tn=tn, tm=tm, rc=rc, seq_len=seq_len,
                    bias=cv_b_in, name="conformer_in_glu")
            s = _conf_mid(v, cv_w_dw, cv_b_dw, cv_ln_g, cv_ln_b, j, seq_len, tconv)
            xf = _mm(s, cv_w_out, j, mode="res_bias", col_offs=(0,), n_cols=d, tn=tn, tm=tm, rc=rc,
                     seq_len=seq_len, res=xf, bias=cv_b_out, name="conformer_out")
        else:
            z = _mm(h, ssm_w_in_t, j, mode="plain", col_offs=(0,), n_cols=d_inner, tn=tn, tm=tm, rc=rc_t,
                    seq_len=seq_len, transposed=True, name="ssm_in_z")
            xbc = _mm(h, ssm_w_in_t, j, mode="conv_silu", col_offs=(d_inner,), n_cols=d_inner + 2 * gn, tn=tn,
                      tm=tm, rc=rc_t, seq_len=seq_len, transposed=True, conv_w=ssm_w_conv, conv_b=ssm_b_conv,
                      name="ssm_in_xbc")
            dt = _mm(h, ssm_w_in_t, j, mode="mask_cols", col_offs=(2 * d_inner + 2 * gn,), n_cols=n_heads,
                     tn=LANES, tm=tm, rc=rc_t, seq_len=seq_len, transposed=True, name="ssm_in_dt")
            yn = _ssd(xbc, z, dt, dt_bias, a_log, d_skip, ssm_norm_g, j, batch, seq_len, tl)
            xf = _mm(yn, ssm_w_out, j, mode="res", col_offs=(0,), n_cols=d, tn=tn, tm=tm_k, rc=rc, seq_len=seq_len,
                     res=xf, name="ssm_out")
        h = _rmsnorm(xf, norm_ffn_g, i, BF16, tnorm)
        act = _mm(h, ffn_w_up, i, mode="ffn", col_offs=(0, f), n_cols=f, tn=tn, tm=tm, rc=rc, seq_len=seq_len,
                  out_dtype=BF16, conv_w=ffn_w_dw, conv_b=ffn_b_dw, name="ffn_up_conv_gate")
        xf = _mm(act, ffn_w_down, i, mode="res", col_offs=(0,), n_cols=d, tn=tn, tm=tm_k, rc=rc, seq_len=seq_len,
                 res=xf, name="ffn_down")
    out = _rmsnorm(xf, norm_final_g, 0, F32, tnorm)
    return out.reshape(batch, seq_len, d)
```

```python
import functools

import jax
import jax.numpy as jnp
from jax import lax
from jax.experimental import pallas as pl
from jax.experimental.pallas import tpu as pltpu

RMS_EPS = 1e-6
LN_EPS = 1e-5

SSD_CHUNK = 64
SSD_HEAD_DIM = 64
SSD_N_GROUPS = 8
SSD_D_STATE = 128
Q_SHIFT = SSD_CHUNK.bit_length() - 1
HEAD_SHIFT = SSD_HEAD_DIM.bit_length() - 1

LANES = 128
SUBLANES = 8
MXU_DIM = 256
VMEM_LIMIT_BYTES = 56 * 1024 * 1024

CONV_HALO = 32
NEG_BIG = -1e30

F32 = jnp.float32
BF16 = jnp.bfloat16


def _params(semantics):
    return pltpu.CompilerParams(dimension_semantics=semantics, vmem_limit_bytes=VMEM_LIMIT_BYTES)


def _rmsnorm_kernel(x_ref, g_ref, o_ref):
    x = x_ref[...]
    ms = jnp.mean(x * x, axis=-1, keepdims=True)
    o_ref[...] = (x * lax.rsqrt(ms + RMS_EPS) * g_ref[...]).astype(o_ref.dtype)


def _rmsnorm(x, gains, layer, out_dtype, tm):
    t, d = x.shape
    return pl.pallas_call(
        _rmsnorm_kernel,
        out_shape=jax.ShapeDtypeStruct((t, d), out_dtype),
        grid=(t // tm,),
        in_specs=[pl.BlockSpec((tm, d), lambda m: (m, 0)),
                  pl.BlockSpec((None, 1, d), lambda m: (layer, 0, 0))],
        out_specs=pl.BlockSpec((tm, d), lambda m: (m, 0)),
        compiler_params=_params(("parallel",)),
        name="rmsnorm",
    )(x, gains)


def _tile_rows(x8, rows):
    return jnp.tile(x8, (rows // SUBLANES, 1))


def _causal_conv_rows(u, tail, p8_ref, kc):
    rc = u.shape[0]
    ext = jnp.concatenate([tail, u], axis=0)
    out = u * _tile_rows(p8_ref[kc - 1], rc) + _tile_rows(p8_ref[kc], rc)
    for k in range(kc - 1):
        start = SUBLANES - (kc - 1 - k)
        out = out + ext[start:start + rc, :] * _tile_rows(p8_ref[k], rc)
    return out


def _mm_kernel(*refs, mode, nw, tiles_per_seq, n_valid, rc, transposed, norm):
    a_ref = refs[0]
    refs = refs[1:]
    if norm:
        gain_ref, rstd_ref = refs[0], refs[-1]
        refs = refs[1:-1]
    w_refs = refs[:nw]
    rest = refs[nw:]
    m = pl.program_id(1)
    tm = a_ref.shape[0]
    conv_refs, p8_refs, carry_ref = (), (), None

    if mode == "glu":
        ba_ref, bg_ref, o_ref, *wb_refs = rest
    elif mode == "res":
        res_ref, o_ref, *wb_refs = rest
    elif mode == "res_bias":
        res_ref, b_ref, o_ref, *wb_refs = rest
    elif mode == "plain" or mode == "mask_cols":
        o_ref, *wb_refs = rest
    elif mode == "conv_silu":
        cw_ref, cb_ref, o_ref, wb0, p8, carry_ref = rest
        wb_refs, conv_refs, p8_refs = [wb0], [(cw_ref, cb_ref)], [p8]
    elif mode == "ffn":
        cwg_ref, cwv_ref, cbg_ref, cbv_ref, o_ref, wb0, wb1, p8g, p8v, carry_ref = rest
        wb_refs, conv_refs, p8_refs = [wb0, wb1], [(cwg_ref, cbg_ref), (cwv_ref, cbv_ref)], [p8g, p8v]
    else:
        raise ValueError(mode)
    kc = conv_refs[0][0].shape[0] if conv_refs else 0

    @pl.when(m == 0)
    def _():
        for w_ref, wb_ref in zip(w_refs, wb_refs):
            w = w_ref[...] * gain_ref[...] if norm else w_ref[...]
            wb_ref[...] = (w.T if transposed else w).astype(BF16)
        for (cw_ref, cb_ref), p8_ref in zip(conv_refs, p8_refs):
            for k in range(kc):
                p8_ref[k] = jnp.broadcast_to(cw_ref[k:k + 1, :], p8_ref.shape[1:])
            p8_ref[kc] = jnp.broadcast_to(cb_ref[...], p8_ref.shape[1:])

    if conv_refs:
        @pl.when((m % tiles_per_seq) == 0)
        def _():
            carry_ref[...] = jnp.zeros(carry_ref.shape, F32)

    def matmul(a, wb_ref):
        return jnp.dot(a, wb_ref[...], preferred_element_type=F32)

    if norm:
        @pl.when(pl.program_id(0) == 0)
        def _():
            x = a_ref[...]
            ms = jnp.mean(x * x, axis=-1, keepdims=True)
            rstd_ref[m] = jnp.broadcast_to(lax.rsqrt(ms + RMS_EPS), rstd_ref.shape[1:])

    def chunk_matmuls(c):
        rows = slice(c * rc, (c + 1) * rc)
        a = a_ref[rows, :].astype(BF16)
        accs = [matmul(a, wb_ref) for wb_ref in wb_refs]
        if norm:
            scale = jnp.tile(rstd_ref[m, rows, :], (1, accs[0].shape[1] // LANES))
            accs = [acc * scale for acc in accs]
        return accs

    tails = [carry_ref[i] for i in range(len(conv_refs))]
    n_chunks = tm // rc
    accs_next = chunk_matmuls(0)
    for c in range(n_chunks):
        rows = slice(c * rc, (c + 1) * rc)
        accs = accs_next
        if c + 1 < n_chunks:
            accs_next = chunk_matmuls(c + 1)
        if mode == "glu":
            o_ref[rows, :] = (accs[0] + ba_ref[...]) * jax.nn.sigmoid(accs[1] + bg_ref[...])
        elif mode == "res":
            o_ref[rows, :] = res_ref[rows, :] + accs[0]
        elif mode == "res_bias":
            o_ref[rows, :] = res_ref[rows, :] + (accs[0] + b_ref[...])
        elif mode == "plain":
            o_ref[rows, :] = accs[0].astype(o_ref.dtype)
        elif mode == "mask_cols":
            lane = lax.broadcasted_iota(jnp.int32, accs[0].shape, 1)
            o_ref[rows, :] = jnp.where(lane < n_valid, accs[0], 0.0)
        else:
            convs = [_causal_conv_rows(u, tail, p8_ref, kc) for u, tail, p8_ref in zip(accs, tails, p8_refs)]
            tails = [u[rc - SUBLANES:rc, :] for u in accs]
            if mode == "conv_silu":
                o_ref[rows, :] = convs[0] * jax.nn.sigmoid(convs[0])
            else:
                o_ref[rows, :] = (convs[0] * jax.nn.sigmoid(convs[0]) * convs[1]).astype(o_ref.dtype)
    for i, tail in enumerate(tails):
        carry_ref[i] = tail


def _mm(a, w, layer, *, mode, col_offs, n_cols, tn, tm, rc, seq_len, out_dtype=F32, transposed=False,
        gain=None, gain_layer=None, bias=None, conv_w=None, conv_b=None, res=None, name):
    t, k = a.shape
    nw = len(col_offs)
    norm = gain is not None
    assert not (norm and transposed)
    assert t % tm == 0 and seq_len % tm == 0 and tm % rc == 0 and all(off % tn == 0 for off in col_offs)
    n_tiles = pl.cdiv(n_cols, tn)
    out_cols = n_tiles * tn
    offs = [off // tn for off in col_offs]

    def col_spec(rows, ob):
        return pl.BlockSpec((None, rows, tn), lambda n, m: (layer, 0, n + ob))

    def weight_spec(ob):
        if transposed:
            return pl.BlockSpec((None, tn, k), lambda n, m: (layer, n + ob, 0))
        return col_spec(k, ob)

    in_specs = [pl.BlockSpec((tm, k), lambda n, m: (m, 0))]
    args = [a]
    if norm:
        in_specs.append(pl.BlockSpec((None, k, 1), lambda n, m: (gain_layer, 0, 0)))
        args.append(gain)
    for ob in offs:
        in_specs.append(weight_spec(ob))
        args.append(w)
    tile_spec = pl.BlockSpec((tm, tn), lambda n, m: (m, n))
    scratch = [pltpu.VMEM((k, tn), BF16) for _ in offs]

    if mode == "glu":
        in_specs += [col_spec(1, ob) for ob in offs]
        args += [bias, bias]
    elif mode == "res":
        in_specs.append(tile_spec)
        args.append(res)
    elif mode == "res_bias":
        in_specs += [tile_spec, col_spec(1, offs[0])]
        args += [res, bias]
    elif mode == "conv_silu":
        kc = conv_w.shape[1]
        in_specs += [col_spec(kc, 0), col_spec(1, 0)]
        args += [conv_w, conv_b]
    elif mode == "ffn":
        kc = conv_w.shape[1]
        in_specs += [col_spec(kc, ob) for ob in offs] + [col_spec(1, ob) for ob in offs]
        args += [conv_w, conv_w, conv_b, conv_b]
    if mode in ("conv_silu", "ffn"):
        assert kc - 1 <= SUBLANES
        scratch += [pltpu.VMEM((kc + 1, SUBLANES, tn), F32) for _ in offs]
        scratch += [pltpu.VMEM((nw, SUBLANES, tn), F32)]
    if norm:
        scratch += [pltpu.VMEM((t // tm, tm, LANES), F32)]

    kern = functools.partial(_mm_kernel, mode=mode, nw=nw, tiles_per_seq=seq_len // tm, n_valid=n_cols,
                             rc=rc, transposed=transposed, norm=norm)
    return pl.pallas_call(
        kern,
        out_shape=jax.ShapeDtypeStruct((t, out_cols), out_dtype),
        grid=(n_tiles, t // tm),
        in_specs=in_specs,
        out_specs=tile_spec,
        scratch_shapes=scratch,
        compiler_params=_params(("arbitrary" if norm else "parallel", "arbitrary")),
        name=name,
    )(*args)


def _conf_mid_kernel(cur_ref, halo_ref, w_ref, b_ref, g_ref, beta_ref, o_ref, ext_ref, conv_ref, w8_ref,
                     *, tm, tiles_per_seq, rc):
    kc = w_ref.shape[0]
    d = cur_ref.shape[1]
    seq_start = (pl.program_id(0) % tiles_per_seq) == 0

    @pl.when(seq_start)
    def _():
        ext_ref[0:CONV_HALO, :] = jnp.zeros((CONV_HALO, d), F32)

    @pl.when(jnp.logical_not(seq_start))
    def _():
        ext_ref[0:CONV_HALO, :] = halo_ref[...]

    ext_ref[CONV_HALO:CONV_HALO + tm, :] = cur_ref[...]
    for k in range(kc):
        w8_ref[k] = jnp.broadcast_to(w_ref[k:k + 1, :], (SUBLANES, d))

    base = CONV_HALO - (kc - 1)
    for cj in range(d // LANES):
        cols = slice(cj * LANES, (cj + 1) * LANES)

        def row_body(ri, carry, cols=cols):
            r0 = pl.multiple_of(ri * rc, rc)
            e = ext_ref[pl.ds(r0, rc + CONV_HALO), cols]
            acc = jnp.zeros((rc, LANES), F32) + b_ref[:, cols]
            for r in range(SUBLANES):
                taps = [k for k in range(kc) if (base + k) % SUBLANES == r]
                if not taps:
                    continue
                n = rc if r == 0 else rc + SUBLANES
                part = None
                for k in taps:
                    lo = (base + k) // SUBLANES * SUBLANES
                    term = e[lo:lo + n, :] * _tile_rows(w8_ref[k, :, cols], n)
                    part = term if part is None else part + term
                acc = acc + part[r:r + rc, :]
            conv_ref[pl.ds(r0, rc), cols] = acc
            return carry

        lax.fori_loop(0, tm // rc, row_body, 0)

    c = conv_ref[...]
    mu = jnp.mean(c, axis=-1, keepdims=True)
    xc = c - mu
    var = jnp.mean(xc * xc, axis=-1, keepdims=True)
    y = xc * lax.rsqrt(var + LN_EPS) * g_ref[...] + beta_ref[...]
    o_ref[...] = (y * jax.nn.sigmoid(y)).astype(o_ref.dtype)


def _conf_mid(v, w_dw, b_dw, ln_g, ln_b, layer, seq_len, tm):
    t, d = v.shape
    kc = w_dw.shape[1]
    assert kc - 1 <= CONV_HALO and seq_len % tm == 0 and tm % CONV_HALO == 0 and d % LANES == 0
    rc = min(64, tm)
    halo_per_tile = tm // CONV_HALO
    row = lambda m: (layer, 0, 0)
    kern = functools.partial(_conf_mid_kernel, tm=tm, tiles_per_seq=seq_len // tm, rc=rc)
    return pl.pallas_call(
        kern,
        out_shape=jax.ShapeDtypeStruct((t, d), BF16),
        grid=(t // tm,),
        in_specs=[pl.BlockSpec((tm, d), lambda m: (m, 0)),
                  pl.BlockSpec((CONV_HALO, d), lambda m: (jnp.maximum(m * halo_per_tile - 1, 0), 0)),
                  pl.BlockSpec((None, kc, d), row),
                  pl.BlockSpec((None, 1, d), row),
                  pl.BlockSpec((None, 1, d), row),
                  pl.BlockSpec((None, 1, d), row)],
        out_specs=pl.BlockSpec((tm, d), lambda m: (m, 0)),
        scratch_shapes=[pltpu.VMEM((CONV_HALO + tm, d), F32), pltpu.VMEM((tm, d), F32),
                        pltpu.VMEM((kc, SUBLANES, d), F32)],
        compiler_params=_params(("parallel",)),
        name="conformer_conv_ln_swish",
    )(v, v, w_dw, b_dw, ln_g, ln_b)


def _split2(x):
    hi = x.astype(BF16)
    lo = (x - hi.astype(F32)).astype(BF16)
    return hi, lo


def _split3(x):
    hi, lo = _split2(x)
    lo2 = (x - hi.astype(F32) - lo.astype(F32)).astype(BF16)
    return hi, lo, lo2


def _ssd_kernel(x_ref, b_ref, c_ref, z_ref, dt_ref, dtb_ref, alog_ref, dskip_ref, ng_ref, o_ref,
                state_ref, parts_ref, dte_ref, acse_ref, *, tl, gw):
    q = SSD_CHUNK
    lt = pl.program_id(1)
    g = pl.program_id(2)
    hpg = gw // SSD_HEAD_DIM

    @pl.when(lt == 0)
    def _():
        state_ref[g] = jnp.zeros(state_ref.shape[1:], F32)

    @pl.when(g == 0)
    def _():
        pre = dt_ref[...] + dtb_ref[...]
        dt = jnp.maximum(pre, 0.0) + jnp.log1p(jnp.exp(-jnp.abs(pre)))
        a = dt * (-jnp.exp(alog_ref[...]))
        r = lax.broadcasted_iota(jnp.int32, (tl, tl), 0)
        c = lax.broadcasted_iota(jnp.int32, (tl, tl), 1)
        tri = jnp.where((r >= c) & ((r >> Q_SHIFT) == (c >> Q_SHIFT)), 1.0, 0.0).astype(BF16)
        acs = jnp.zeros((tl, LANES), F32)
        for part in _split3(a):
            acs = acs + jnp.dot(tri, part, preferred_element_type=F32)
        for i, part in enumerate(_split2(dt) + _split2(acs)):
            parts_ref[i * tl:(i + 1) * tl, :] = part

    hrow = lax.broadcasted_iota(jnp.int32, (LANES, gw), 0)
    hcol = lax.broadcasted_iota(jnp.int32, (LANES, gw), 1)
    expand = jnp.where(hrow == g * hpg + (hcol >> HEAD_SHIFT), 1.0, 0.0).astype(BF16)
    ex = jnp.dot(parts_ref[...], expand, preferred_element_type=F32)
    dte_ref[...] = ex[0:tl] + ex[tl:2 * tl]
    acse_ref[...] = ex[2 * tl:3 * tl] + ex[3 * tl:4 * tl]

    row = lax.broadcasted_iota(jnp.int32, (q, gw), 0)
    pos = lax.broadcasted_iota(jnp.int32, (q, gw), 1) & (SSD_HEAD_DIM - 1)
    causal = row >= pos
    diagonal = row == pos
    br = lax.broadcasted_iota(jnp.int32, (MXU_DIM, MXU_DIM), 0) >> HEAD_SHIFT
    bc = lax.broadcasted_iota(jnp.int32, (MXU_DIM, MXU_DIM), 1) >> HEAD_SHIFT
    same_head = br == bc
    heads_per_dot = MXU_DIM // SSD_HEAD_DIM
    dskip = dskip_ref[...]
    ng = ng_ref[...]

    for ci in range(tl // q):
        r0 = ci * q
        x = x_ref[pl.ds(r0, q), :]
        bm = b_ref[pl.ds(r0, q), :].astype(BF16)
        cm = c_ref[pl.ds(r0, q), :].astype(BF16)
        acs = acse_ref[pl.ds(r0, q), :]
        a_last = acse_ref[pl.ds(r0 + q - 1, 1), :]
        xd = x * dte_ref[pl.ds(r0, q), :]
        xd_bf = xd.astype(BF16)

        prev = state_ref[g]
        y = jnp.dot(cm, prev.astype(BF16), preferred_element_type=F32) * jnp.exp(acs)

        wgt = (jnp.exp(a_last - acs) * xd).astype(BF16)
        s_new = lax.dot_general(bm, wgt, (((0,), (0,)), ((), ())), preferred_element_type=F32)
        state_ref[g] = prev * jnp.exp(a_last) + s_new

        cb = lax.dot_general(cm, jnp.concatenate([bm] * hpg, axis=0), (((1,), (1,)), ((), ())),
                             preferred_element_type=F32)
        acs_s = jnp.sum(jnp.where(diagonal, acs, 0.0), axis=0, keepdims=True)
        decay = jnp.exp(jnp.where(causal, acs - acs_s, NEG_BIG))
        mix = (cb * decay).astype(BF16)
        parts = []
        for j in range(gw // MXU_DIM):
            cols = slice(j * MXU_DIM, (j + 1) * MXU_DIM)
            blockdiag = jnp.where(same_head, jnp.concatenate([xd_bf[:, cols]] * heads_per_dot, axis=0),
                                  jnp.zeros((), BF16))
            parts.append(jnp.dot(mix[:, cols], blockdiag, preferred_element_type=F32))
        y = y + jnp.concatenate(parts, axis=1) + dskip * x

        z = z_ref[pl.ds(r0, q), :]
        yg = y * (z * jax.nn.sigmoid(z))
        ms = jnp.mean(yg * yg, axis=-1, keepdims=True)
        o_ref[pl.ds(r0, q), :] = (yg * lax.rsqrt(ms + RMS_EPS) * ng).astype(o_ref.dtype)


def _ssd(xbc, z, dt, dt_bias, a_log, d_skip, norm_g, layer, batch, seq_len, tl):
    t, d_inner = z.shape
    gw = d_inner // SSD_N_GROUPS
    assert gw % MXU_DIM == 0 and SSD_D_STATE == LANES and seq_len % tl == 0 and tl % SSD_CHUNK == 0
    nl = seq_len // tl
    b_blk = d_inner // SSD_D_STATE
    c_blk = b_blk + SSD_N_GROUPS
    rows = lambda b, l, g: (b * nl + l, g)
    head_row = lambda b, l, g: (layer, 0, 0)
    group_row = lambda b, l, g: (layer, 0, g)
    kern = functools.partial(_ssd_kernel, tl=tl, gw=gw)
    return pl.pallas_call(
        kern,
        out_shape=jax.ShapeDtypeStruct((t, d_inner), BF16),
        grid=(batch, nl, SSD_N_GROUPS),
        in_specs=[pl.BlockSpec((tl, gw), rows),
                  pl.BlockSpec((tl, SSD_D_STATE), lambda b, l, g: (b * nl + l, b_blk + g)),
                  pl.BlockSpec((tl, SSD_D_STATE), lambda b, l, g: (b * nl + l, c_blk + g)),
                  pl.BlockSpec((tl, gw), rows),
                  pl.BlockSpec((tl, LANES), lambda b, l, g: (b * nl + l, 0)),
                  pl.BlockSpec((None, 1, LANES), head_row),
                  pl.BlockSpec((None, 1, LANES), head_row),
                  pl.BlockSpec((None, 1, gw), group_row),
                  pl.BlockSpec((None, 1, gw), group_row)],
        out_specs=pl.BlockSpec((tl, gw), rows),
        scratch_shapes=[pltpu.VMEM((SSD_N_GROUPS, SSD_D_STATE, gw), F32),
                        pltpu.VMEM((4 * tl, LANES), BF16),
                        pltpu.VMEM((tl, gw), F32),
                        pltpu.VMEM((tl, gw), F32)],
        compiler_params=_params(("parallel", "arbitrary", "arbitrary")),
        name="ssd_scan_gate_norm",
    )(xbc, xbc, xbc, z, dt, dt_bias, a_log, d_skip, norm_g)


def _row3(p):
    return p.reshape(p.shape[0], 1, p.shape[1])


def _pad_lanes(p):
    return jnp.pad(p, ((0, 0), (0, LANES - p.shape[1])))


def kernel(x, norm_mix_g, norm_ffn_g, norm_final_g, cv_w_in, cv_b_in, cv_w_dw, cv_b_dw, cv_ln_g, cv_ln_b, cv_w_out, cv_b_out, ssm_w_in, ssm_w_conv, ssm_b_conv, ssm_dt_bias, ssm_a_log, ssm_d, ssm_norm_g, ssm_w_out, ffn_w_up, ffn_w_dw, ffn_b_dw, ffn_w_down):
    batch, seq_len, d = x.shape
    t = batch * seq_len
    depth = norm_mix_g.shape[0]
    d_inner = ssm_w_out.shape[1]
    n_heads = ssm_dt_bias.shape[1]
    gn = SSD_N_GROUPS * SSD_D_STATE
    f = ffn_w_down.shape[1]
    assert n_heads * SSD_HEAD_DIM == d_inner and n_heads <= LANES
    assert ssm_w_in.shape[2] == 2 * d_inner + 2 * gn + n_heads

    tm = min(1024, seq_len)
    tm2 = min(2048, seq_len)
    tm_k = min(512, seq_len)
    tm_s = min(256, seq_len)
    tn = min(512, d)
    tn2 = min(1024, d)
    tnorm = min(512, seq_len)
    tconv = min(256, seq_len)
    tl = min(512, seq_len)
    rc = min(256, seq_len)
    rc_t = min(512, seq_len)

    mix_gain = norm_mix_g.reshape(depth, d, 1)
    ffn_gain = norm_ffn_g.reshape(depth, d, 1)
    norm_mix_g = _row3(norm_mix_g)
    norm_final_g = norm_final_g.reshape(1, 1, d)
    cv_b_in, cv_b_dw, cv_ln_g, cv_ln_b, cv_b_out = map(_row3, (cv_b_in, cv_b_dw, cv_ln_g, cv_ln_b, cv_b_out))
    ssm_b_conv, ssm_norm_g, ffn_b_dw = map(_row3, (ssm_b_conv, ssm_norm_g, ffn_b_dw))
    dt_bias = _row3(_pad_lanes(ssm_dt_bias))
    a_log = _row3(_pad_lanes(ssm_a_log))
    d_skip = _row3(jnp.repeat(ssm_d, SSD_HEAD_DIM, axis=1))
    ssm_w_in_t = jnp.swapaxes(ssm_w_in, 1, 2)

    xf = x.reshape(t, d)
    for i in range(depth):
        j = i // 2
        if i % 2 == 0:
            v = _mm(xf, cv_w_in, j, mode="glu", col_offs=(0, d), n_cols=d, tn=tn, tm=tm, rc=rc, seq_len=seq_len,
                    gain=mix_gain, gain_layer=i, bias=cv_b_in, name="conformer_in_glu")
            s = _conf_mid(v, cv_w_dw, cv_b_dw, cv_ln_g, cv_ln_b, j, seq_len, tconv)
            xf = _mm(s, cv_w_out, j, mode="res_bias", col_offs=(0,), n_cols=d, tn=tn2, tm=tm, rc=rc,
                     seq_len=seq_len, res=xf, bias=cv_b_out, name="conformer_out")
        else:
            h = _rmsnorm(xf, norm_mix_g, i, BF16, tnorm)
            z = _mm(h, ssm_w_in_t, j, mode="plain", col_offs=(0,), n_cols=d_inner, tn=tn2, tm=tm, rc=rc_t,
                    seq_len=seq_len, transposed=True, name="ssm_in_z")
            xbc = _mm(h, ssm_w_in_t, j, mode="conv_silu", col_offs=(d_inner,), n_cols=d_inner + 2 * gn, tn=tn,
                      tm=tm2, rc=rc, seq_len=seq_len, transposed=True, conv_w=ssm_w_conv, conv_b=ssm_b_conv,
                      name="ssm_in_xbc")
            dt = _mm(h, ssm_w_in_t, j, mode="mask_cols", col_offs=(2 * d_inner + 2 * gn,), n_cols=n_heads,
                     tn=LANES, tm=tm, rc=rc_t, seq_len=seq_len, transposed=True, name="ssm_in_dt")
            yn = _ssd(xbc, z, dt, dt_bias, a_log, d_skip, ssm_norm_g, j, batch, seq_len, tl)
            xf = _mm(yn, ssm_w_out, j, mode="res", col_offs=(0,), n_cols=d, tn=tn2, tm=tm_s, rc=tm_s,
                     seq_len=seq_len, res=xf, name="ssm_out")
        act = _mm(xf, ffn_w_up, i, mode="ffn", col_offs=(0, f), n_cols=f, tn=tn, tm=tm, rc=rc, seq_len=seq_len,
                  out_dtype=BF16, gain=ffn_gain, gain_layer=i, conv_w=ffn_w_dw, conv_b=ffn_b_dw,
                  name="ffn_up_conv_gate")
        xf = _mm(act, ffn_w_down, i, mode="res", col_offs=(0,), n_cols=d, tn=tn, tm=tm_k, rc=rc, seq_len=seq_len,
                 res=xf, name="ffn_down")
    out = _rmsnorm(xf, norm_final_g, 0, F32, tnorm)
    return out.reshape(batch, seq_len, d)
```

```python
import functools

import jax
import jax.numpy as jnp
from jax import lax
from jax.experimental import pallas as pl
from jax.experimental.pallas import tpu as pltpu

RMS_EPS = 1e-6
LN_EPS = 1e-5

SSD_CHUNK = 64
SSD_HEAD_DIM = 64
SSD_N_GROUPS = 8
SSD_D_STATE = 128
SSD_GROUPS_PER_STEP = 2
Q_SHIFT = SSD_CHUNK.bit_length() - 1
HEAD_SHIFT = SSD_HEAD_DIM.bit_length() - 1

LANES = 128
SUBLANES = 8
MXU_DIM = 256
VMEM_LIMIT_BYTES = 56 * 1024 * 1024

CONV_HALO = 32
NEG_BIG = -1e30
LOG2_E = 1.4426950408889634

F32 = jnp.float32
BF16 = jnp.bfloat16


def _params(semantics):
    return pltpu.CompilerParams(dimension_semantics=semantics, vmem_limit_bytes=VMEM_LIMIT_BYTES)


def _rmsnorm_kernel(x_ref, g_ref, o_ref):
    x = x_ref[...]
    ms = jnp.mean(x * x, axis=-1, keepdims=True)
    o_ref[...] = (x * lax.rsqrt(ms + RMS_EPS) * g_ref[...]).astype(o_ref.dtype)


def _rmsnorm(x, gains, layer, out_dtype, tm):
    t, d = x.shape
    return pl.pallas_call(
        _rmsnorm_kernel,
        out_shape=jax.ShapeDtypeStruct((t, d), out_dtype),
        grid=(t // tm,),
        in_specs=[pl.BlockSpec((tm, d), lambda m: (m, 0)),
                  pl.BlockSpec((None, 1, d), lambda m: (layer, 0, 0))],
        out_specs=pl.BlockSpec((tm, d), lambda m: (m, 0)),
        compiler_params=_params(("parallel",)),
        name="rmsnorm",
    )(x, gains)


def _tile_rows(x8, rows):
    return jnp.tile(x8, (rows // SUBLANES, 1))


def _causal_conv_rows(u, tail, p8_ref, kc):
    rc = u.shape[0]
    ext = jnp.concatenate([tail, u], axis=0)
    out = u * _tile_rows(p8_ref[kc - 1], rc) + _tile_rows(p8_ref[kc], rc)
    for k in range(kc - 1):
        start = SUBLANES - (kc - 1 - k)
        out = out + ext[start:start + rc, :] * _tile_rows(p8_ref[k], rc)
    return out


def _mm_kernel(*refs, mode, nw, tiles_per_seq, n_valid, rc, transposed, norm):
    a_ref = refs[0]
    refs = refs[1:]
    if norm:
        gain_ref, rstd_ref = refs[0], refs[-1]
        refs = refs[1:-1]
    w_refs = refs[:nw]
    rest = refs[nw:]
    m = pl.program_id(1)
    tm = a_ref.shape[0]
    tn = w_refs[0].shape[0] if transposed else w_refs[0].shape[1]
    conv_refs, p8_refs, carry_ref = (), (), None

    if mode == "glu":
        ba_ref, bg_ref, o_ref, wb_ref = rest
    elif mode == "res":
        res_ref, o_ref, wb_ref = rest
    elif mode == "res_bias":
        res_ref, b_ref, o_ref, wb_ref = rest
    elif mode == "plain" or mode == "mask_cols":
        o_ref, wb_ref = rest
    elif mode == "conv_silu":
        cw_ref, cb_ref, o_ref, wb_ref, p8, carry_ref = rest
        conv_refs, p8_refs = [(cw_ref, cb_ref)], [p8]
    elif mode == "ffn":
        cwg_ref, cwv_ref, cbg_ref, cbv_ref, o_ref, wb_ref, p8g, p8v, carry_ref = rest
        conv_refs, p8_refs = [(cwg_ref, cbg_ref), (cwv_ref, cbv_ref)], [p8g, p8v]
    else:
        raise ValueError(mode)
    kc = conv_refs[0][0].shape[0] if conv_refs else 0

    @pl.when(m == 0)
    def _():
        for i, w_ref in enumerate(w_refs):
            w = w_ref[...] * gain_ref[...] if norm else w_ref[...]
            wb_ref[:, i * tn:(i + 1) * tn] = (w.T if transposed else w).astype(BF16)
        for (cw_ref, cb_ref), p8_ref in zip(conv_refs, p8_refs):
            for k in range(kc):
                p8_ref[k] = jnp.broadcast_to(cw_ref[k:k + 1, :], p8_ref.shape[1:])
            p8_ref[kc] = jnp.broadcast_to(cb_ref[...], p8_ref.shape[1:])

    if conv_refs:
        @pl.when((m % tiles_per_seq) == 0)
        def _():
            carry_ref[...] = jnp.zeros(carry_ref.shape, F32)

    if norm:
        @pl.when(pl.program_id(0) == 0)
        def _():
            x = a_ref[...]
            ms = jnp.mean(x * x, axis=-1, keepdims=True)
            rstd_ref[m] = jnp.broadcast_to(lax.rsqrt(ms + RMS_EPS), rstd_ref.shape[1:])

    def chunk_matmuls(c):
        rows = slice(c * rc, (c + 1) * rc)
        a = a_ref[rows, :].astype(BF16)
        acc = jnp.dot(a, wb_ref[...], preferred_element_type=F32)
        accs = [acc[:, i * tn:(i + 1) * tn] for i in range(nw)]
        if norm:
            scale = jnp.tile(rstd_ref[m, rows, :], (1, accs[0].shape[1] // LANES))
            accs = [acc * scale for acc in accs]
        return accs

    tails = [carry_ref[i] for i in range(len(conv_refs))]
    n_chunks = tm // rc
    accs_next = chunk_matmuls(0)
    for c in range(n_chunks):
        rows = slice(c * rc, (c + 1) * rc)
        accs = accs_next
        if c + 1 < n_chunks:
            accs_next = chunk_matmuls(c + 1)
        if mode == "glu":
            o_ref[rows, :] = (accs[0] + ba_ref[...]) * jax.nn.sigmoid(accs[1] + bg_ref[...])
        elif mode == "res":
            o_ref[rows, :] = res_ref[rows, :] + accs[0]
        elif mode == "res_bias":
            o_ref[rows, :] = res_ref[rows, :] + (accs[0] + b_ref[...])
        elif mode == "plain":
            o_ref[rows, :] = accs[0].astype(o_ref.dtype)
        elif mode == "mask_cols":
            lane = lax.broadcasted_iota(jnp.int32, accs[0].shape, 1)
            o_ref[rows, :] = jnp.where(lane < n_valid, accs[0], 0.0)
        else:
            convs = [_causal_conv_rows(u, tail, p8_ref, kc) for u, tail, p8_ref in zip(accs, tails, p8_refs)]
            tails = [u[rc - SUBLANES:rc, :] for u in accs]
            if mode == "conv_silu":
                o_ref[rows, :] = convs[0] * jax.nn.sigmoid(convs[0])
            else:
                o_ref[rows, :] = (convs[0] * jax.nn.sigmoid(convs[0]) * convs[1]).astype(o_ref.dtype)
    for i, tail in enumerate(tails):
        carry_ref[i] = tail


def _mm(a, w, layer, *, mode, col_offs, n_cols, tn, tm, rc, seq_len, out_dtype=F32, transposed=False,
        gain=None, gain_layer=None, bias=None, conv_w=None, conv_b=None, res=None, name):
    t, k = a.shape
    nw = len(col_offs)
    norm = gain is not None
    assert not (norm and transposed)
    assert t % tm == 0 and seq_len % tm == 0 and tm % rc == 0 and all(off % tn == 0 for off in col_offs)
    n_tiles = pl.cdiv(n_cols, tn)
    out_cols = n_tiles * tn
    offs = [off // tn for off in col_offs]

    def col_spec(rows, ob):
        return pl.BlockSpec((None, rows, tn), lambda n, m: (layer, 0, n + ob))

    def weight_spec(ob):
        if transposed:
            return pl.BlockSpec((None, tn, k), lambda n, m: (layer, n + ob, 0))
        return col_spec(k, ob)

    in_specs = [pl.BlockSpec((tm, k), lambda n, m: (m, 0))]
    args = [a]
    if norm:
        in_specs.append(pl.BlockSpec((None, k, 1), lambda n, m: (gain_layer, 0, 0)))
        args.append(gain)
    for ob in offs:
        in_specs.append(weight_spec(ob))
        args.append(w)
    tile_spec = pl.BlockSpec((tm, tn), lambda n, m: (m, n))
    scratch = [pltpu.VMEM((k, nw * tn), BF16)]

    if mode == "glu":
        in_specs += [col_spec(1, ob) for ob in offs]
        args += [bias, bias]
    elif mode == "res":
        in_specs.append(tile_spec)
        args.append(res)
    elif mode == "res_bias":
        in_specs += [tile_spec, col_spec(1, offs[0])]
        args += [res, bias]
    elif mode == "conv_silu":
        kc = conv_w.shape[1]
        in_specs += [col_spec(kc, 0), col_spec(1, 0)]
        args += [conv_w, conv_b]
    elif mode == "ffn":
        kc = conv_w.shape[1]
        in_specs += [col_spec(kc, ob) for ob in offs] + [col_spec(1, ob) for ob in offs]
        args += [conv_w, conv_w, conv_b, conv_b]
    if mode in ("conv_silu", "ffn"):
        assert kc - 1 <= SUBLANES
        scratch += [pltpu.VMEM((kc + 1, SUBLANES, tn), F32) for _ in offs]
        scratch += [pltpu.VMEM((nw, SUBLANES, tn), F32)]
    if norm:
        scratch += [pltpu.VMEM((t // tm, tm, LANES), F32)]

    kern = functools.partial(_mm_kernel, mode=mode, nw=nw, tiles_per_seq=seq_len // tm, n_valid=n_cols,
                             rc=rc, transposed=transposed, norm=norm)
    return pl.pallas_call(
        kern,
        out_shape=jax.ShapeDtypeStruct((t, out_cols), out_dtype),
        grid=(n_tiles, t // tm),
        in_specs=in_specs,
        out_specs=tile_spec,
        scratch_shapes=scratch,
        compiler_params=_params(("arbitrary" if norm else "parallel", "arbitrary")),
        name=name,
    )(*args)


def _conf_mid_kernel(cur_ref, halo_ref, w_ref, b_ref, g_ref, beta_ref, o_ref, ext_ref, conv_ref, w8_ref,
                     *, tm, tiles_per_seq, rc):
    kc = w_ref.shape[0]
    d = cur_ref.shape[1]
    seq_start = (pl.program_id(0) % tiles_per_seq) == 0

    @pl.when(seq_start)
    def _():
        ext_ref[0:CONV_HALO, :] = jnp.zeros((CONV_HALO, d), F32)

    @pl.when(jnp.logical_not(seq_start))
    def _():
        ext_ref[0:CONV_HALO, :] = halo_ref[...]

    ext_ref[CONV_HALO:CONV_HALO + tm, :] = cur_ref[...]
    for k in range(kc):
        w8_ref[k] = jnp.broadcast_to(w_ref[k:k + 1, :], (SUBLANES, d))

    base = CONV_HALO - (kc - 1)
    for cj in range(d // LANES):
        cols = slice(cj * LANES, (cj + 1) * LANES)

        def row_body(ri, carry, cols=cols):
            r0 = pl.multiple_of(ri * rc, rc)
            e = ext_ref[pl.ds(r0, rc + CONV_HALO), cols]
            acc = jnp.zeros((rc, LANES), F32) + b_ref[:, cols]
            for r in range(SUBLANES):
                taps = [k for k in range(kc) if (base + k) % SUBLANES == r]
                if not taps:
                    continue
                n = rc if r == 0 else rc + SUBLANES
                part = None
                for k in taps:
                    lo = (base + k) // SUBLANES * SUBLANES
                    term = e[lo:lo + n, :] * _tile_rows(w8_ref[k, :, cols], n)
                    part = term if part is None else part + term
                acc = acc + part[r:r + rc, :]
            conv_ref[pl.ds(r0, rc), cols] = acc
            return carry

        lax.fori_loop(0, tm // rc, row_body, 0)

    c = conv_ref[...]
    mu = jnp.mean(c, axis=-1, keepdims=True)
    xc = c - mu
    var = jnp.mean(xc * xc, axis=-1, keepdims=True)
    y = xc * lax.rsqrt(var + LN_EPS) * g_ref[...] + beta_ref[...]
    o_ref[...] = (y * jax.nn.sigmoid(y)).astype(o_ref.dtype)


def _conf_mid(v, w_dw, b_dw, ln_g, ln_b, layer, seq_len, tm):
    t, d = v.shape
    kc = w_dw.shape[1]
    assert kc - 1 <= CONV_HALO and seq_len % tm == 0 and tm % CONV_HALO == 0 and d % LANES == 0
    rc = min(64, tm)
    halo_per_tile = tm // CONV_HALO
    row = lambda m: (layer, 0, 0)
    kern = functools.partial(_conf_mid_kernel, tm=tm, tiles_per_seq=seq_len // tm, rc=rc)
    return pl.pallas_call(
        kern,
        out_shape=jax.ShapeDtypeStruct((t, d), BF16),
        grid=(t // tm,),
        in_specs=[pl.BlockSpec((tm, d), lambda m: (m, 0)),
                  pl.BlockSpec((CONV_HALO, d), lambda m: (jnp.maximum(m * halo_per_tile - 1, 0), 0)),
                  pl.BlockSpec((None, kc, d), row),
                  pl.BlockSpec((None, 1, d), row),
                  pl.BlockSpec((None, 1, d), row),
                  pl.BlockSpec((None, 1, d), row)],
        out_specs=pl.BlockSpec((tm, d), lambda m: (m, 0)),
        scratch_shapes=[pltpu.VMEM((CONV_HALO + tm, d), F32), pltpu.VMEM((tm, d), F32),
                        pltpu.VMEM((kc, SUBLANES, d), F32)],
        compiler_params=_params(("parallel",)),
        name="conformer_conv_ln_swish",
    )(v, v, w_dw, b_dw, ln_g, ln_b)


def _split2(x):
    hi = x.astype(BF16)
    lo = (x - hi.astype(F32)).astype(BF16)
    return hi, lo


def _split3(x):
    hi, lo = _split2(x)
    lo2 = (x - hi.astype(F32) - lo.astype(F32)).astype(BF16)
    return hi, lo, lo2


def _ssd_kernel(x_ref, b_ref, c_ref, z_ref, dt_ref, dtb_ref, alog_ref, dskip_ref, ng_ref, o_ref,
                state_ref, parts_ref, dte_ref, acse_ref, *, tl, gw, gs):
    q = SSD_CHUNK
    lt = pl.program_id(1)
    g0 = pl.program_id(2) * gs
    hpg = gw // SSD_HEAD_DIM

    @pl.when(lt == 0)
    def _():
        for gi in range(gs):
            state_ref[g0 + gi] = jnp.zeros(state_ref.shape[1:], F32)

    @pl.when(g0 == 0)
    def _():
        pre = dt_ref[...] + dtb_ref[...]
        dt = jnp.maximum(pre, 0.0) + jnp.log1p(jnp.exp(-jnp.abs(pre)))
        a = dt * (-jnp.exp(alog_ref[...]) * LOG2_E)
        r = lax.broadcasted_iota(jnp.int32, (tl, tl), 0)
        c = lax.broadcasted_iota(jnp.int32, (tl, tl), 1)
        tri = jnp.where((r >= c) & ((r >> Q_SHIFT) == (c >> Q_SHIFT)), 1.0, 0.0).astype(BF16)
        acs = jnp.zeros((tl, LANES), F32)
        for part in _split3(a):
            acs = acs + jnp.dot(tri, part, preferred_element_type=F32)
        for i, part in enumerate(_split2(dt) + _split2(acs)):
            parts_ref[i * tl:(i + 1) * tl, :] = part

    hrow = lax.broadcasted_iota(jnp.int32, (LANES, gs * gw), 0)
    hcol = lax.broadcasted_iota(jnp.int32, (LANES, gs * gw), 1)
    expand = jnp.where(hrow == g0 * hpg + (hcol >> HEAD_SHIFT), 1.0, 0.0).astype(BF16)
    ex = jnp.dot(parts_ref[...], expand, preferred_element_type=F32)
    dte_ref[...] = ex[0:tl] + ex[tl:2 * tl]
    acse_ref[...] = ex[2 * tl:3 * tl] + ex[3 * tl:4 * tl]

    row = lax.broadcasted_iota(jnp.int32, (q, gw), 0)
    pos = lax.broadcasted_iota(jnp.int32, (q, gw), 1) & (SSD_HEAD_DIM - 1)
    causal = row >= pos
    diagonal = row == pos
    br = lax.broadcasted_iota(jnp.int32, (MXU_DIM, MXU_DIM), 0) >> HEAD_SHIFT
    bc = lax.broadcasted_iota(jnp.int32, (MXU_DIM, MXU_DIM), 1) >> HEAD_SHIFT
    head_mask = jnp.where(br == bc, 1.0, 0.0).astype(BF16)
    heads_per_dot = MXU_DIM // SSD_HEAD_DIM

    for ci in range(tl // q):
        rows = pl.ds(ci * q, q)
        for gi in range(gs):
            lanes = slice(gi * gw, (gi + 1) * gw)
            nlanes = slice(gi * SSD_D_STATE, (gi + 1) * SSD_D_STATE)
            x = x_ref[rows, lanes]
            bm = b_ref[rows, nlanes].astype(BF16)
            cm = c_ref[rows, nlanes].astype(BF16)
            acs = acse_ref[rows, lanes]
            a_last = acse_ref[pl.ds(ci * q + q - 1, 1), lanes]
            xd = x * dte_ref[rows, lanes]
            xd_bf = xd.astype(BF16)

            prev = state_ref[g0 + gi]
            y = jnp.dot(cm, prev.astype(BF16), preferred_element_type=F32) * jnp.exp2(acs)

            wgt = (jnp.exp2(a_last - acs) * xd).astype(BF16)
            s_new = lax.dot_general(bm, wgt, (((0,), (0,)), ((), ())), preferred_element_type=F32)
            state_ref[g0 + gi] = prev * jnp.exp2(a_last) + s_new

            cb = lax.dot_general(cm, jnp.concatenate([bm] * hpg, axis=0), (((1,), (1,)), ((), ())),
                                 preferred_element_type=F32)
            acs_s = jnp.sum(jnp.where(diagonal, acs, 0.0), axis=0, keepdims=True)
            decay = jnp.exp2(jnp.where(causal, acs - acs_s, NEG_BIG))
            mix = (cb * decay).astype(BF16)
            parts = []
            for j in range(gw // MXU_DIM):
                cols = slice(j * MXU_DIM, (j + 1) * MXU_DIM)
                blockdiag = jnp.concatenate([xd_bf[:, cols]] * heads_per_dot, axis=0) * head_mask
                parts.append(jnp.dot(mix[:, cols], blockdiag, preferred_element_type=F32))
            y = y + jnp.concatenate(parts, axis=1) + dskip_ref[:, lanes] * x

            z = z_ref[rows, lanes]
            yg = y * (z * jax.nn.sigmoid(z))
            ms = jnp.mean(yg * yg, axis=-1, keepdims=True)
            o_ref[rows, lanes] = (yg * lax.rsqrt(ms + RMS_EPS) * ng_ref[:, lanes]).astype(o_ref.dtype)


def _ssd(xbc, z, dt, dt_bias, a_log, d_skip, norm_g, layer, batch, seq_len, tl, gs):
    t, d_inner = z.shape
    gw = d_inner // SSD_N_GROUPS
    bw = gs * SSD_D_STATE
    assert gw % MXU_DIM == 0 and SSD_D_STATE == LANES and seq_len % tl == 0 and tl % SSD_CHUNK == 0
    assert SSD_N_GROUPS % gs == 0 and d_inner % bw == 0
    nl = seq_len // tl
    b_blk = d_inner // bw
    c_blk = b_blk + SSD_N_GROUPS // gs
    rows = lambda b, l, g: (b * nl + l, g)
    head_row = lambda b, l, g: (layer, 0, 0)
    group_row = lambda b, l, g: (layer, 0, g)
    kern = functools.partial(_ssd_kernel, tl=tl, gw=gw, gs=gs)
    return pl.pallas_call(
        kern,
        out_shape=jax.ShapeDtypeStruct((t, d_inner), BF16),
        grid=(batch, nl, SSD_N_GROUPS // gs),
        in_specs=[pl.BlockSpec((tl, gs * gw), rows),
                  pl.BlockSpec((tl, bw), lambda b, l, g: (b * nl + l, b_blk + g)),
                  pl.BlockSpec((tl, bw), lambda b, l, g: (b * nl + l, c_blk + g)),
                  pl.BlockSpec((tl, gs * gw), rows),
                  pl.BlockSpec((tl, LANES), lambda b, l, g: (b * nl + l, 0)),
                  pl.BlockSpec((None, 1, LANES), head_row),
                  pl.BlockSpec((None, 1, LANES), head_row),
                  pl.BlockSpec((None, 1, gs * gw), group_row),
                  pl.BlockSpec((None, 1, gs * gw), group_row)],
        out_specs=pl.BlockSpec((tl, gs * gw), rows),
        scratch_shapes=[pltpu.VMEM((SSD_N_GROUPS, SSD_D_STATE, gw), F32),
                        pltpu.VMEM((4 * tl, LANES), BF16),
                        pltpu.VMEM((tl, gs * gw), F32),
                        pltpu.VMEM((tl, gs * gw), F32)],
        compiler_params=_params(("parallel", "arbitrary", "arbitrary")),
        name="ssd_scan_gate_norm",
    )(xbc, xbc, xbc, z, dt, dt_bias, a_log, d_skip, norm_g)


def _row3(p):
    return p.reshape(p.shape[0], 1, p.shape[1])


def _pad_lanes(p):
    return jnp.pad(p, ((0, 0), (0, LANES - p.shape[1])))


def kernel(x, norm_mix_g, norm_ffn_g, norm_final_g, cv_w_in, cv_b_in, cv_w_dw, cv_b_dw, cv_ln_g, cv_ln_b, cv_w_out, cv_b_out, ssm_w_in, ssm_w_conv, ssm_b_conv, ssm_dt_bias, ssm_a_log, ssm_d, ssm_norm_g, ssm_w_out, ffn_w_up, ffn_w_dw, ffn_b_dw, ffn_w_down):
    batch, seq_len, d = x.shape
    t = batch * seq_len
    depth = norm_mix_g.shape[0]
    d_inner = ssm_w_out.shape[1]
    n_heads = ssm_dt_bias.shape[1]
    gn = SSD_N_GROUPS * SSD_D_STATE
    f = ffn_w_down.shape[1]
    assert n_heads * SSD_HEAD_DIM == d_inner and n_heads <= LANES
    assert ssm_w_in.shape[2] == 2 * d_inner + 2 * gn + n_heads

    tm = min(1024, seq_len)
    tm2 = min(2048, seq_len)
    tm_k = min(512, seq_len)
    tm_s = min(256, seq_len)
    tn = min(512, d)
    tn2 = min(1024, d)
    tnorm = min(512, seq_len)
    tconv = min(256, seq_len)
    tl = min(512, seq_len)
    rc = min(256, seq_len)
    rc_t = min(512, seq_len)

    mix_gain = norm_mix_g.reshape(depth, d, 1)
    norm_mix_g, norm_ffn_g = _row3(norm_mix_g), _row3(norm_ffn_g)
    norm_final_g = norm_final_g.reshape(1, 1, d)
    cv_b_in, cv_b_dw, cv_ln_g, cv_ln_b, cv_b_out = map(_row3, (cv_b_in, cv_b_dw, cv_ln_g, cv_ln_b, cv_b_out))
    ssm_b_conv, ssm_norm_g, ffn_b_dw = map(_row3, (ssm_b_conv, ssm_norm_g, ffn_b_dw))
    dt_bias = _row3(_pad_lanes(ssm_dt_bias))
    a_log = _row3(_pad_lanes(ssm_a_log))
    d_skip = _row3(jnp.repeat(ssm_d, SSD_HEAD_DIM, axis=1))
    ssm_w_in_t = jnp.swapaxes(ssm_w_in, 1, 2)

    xf = x.reshape(t, d)
    for i in range(depth):
        j = i // 2
        if i % 2 == 0:
            v = _mm(xf, cv_w_in, j, mode="glu", col_offs=(0, d), n_cols=d, tn=tn, tm=tm, rc=rc, seq_len=seq_len,
                    gain=mix_gain, gain_layer=i, bias=cv_b_in, name="conformer_in_glu")
            s = _conf_mid(v, cv_w_dw, cv_b_dw, cv_ln_g, cv_ln_b, j, seq_len, tconv)
            xf = _mm(s, cv_w_out, j, mode="res_bias", col_offs=(0,), n_cols=d, tn=tn2, tm=tm, rc=rc,
                     seq_len=seq_len, res=xf, bias=cv_b_out, name="conformer_out")
        else:
            h = _rmsnorm(xf, norm_mix_g, i, BF16, tnorm)
            z = _mm(h, ssm_w_in_t, j, mode="plain", col_offs=(0,), n_cols=d_inner, tn=tn2, tm=tm, rc=rc_t,
                    seq_len=seq_len, transposed=True, name="ssm_in_z")
            xbc = _mm(h, ssm_w_in_t, j, mode="conv_silu", col_offs=(d_inner,), n_cols=d_inner + 2 * gn, tn=tn2,
                      tm=tm, rc=rc, seq_len=seq_len, transposed=True, conv_w=ssm_w_conv, conv_b=ssm_b_conv,
                      name="ssm_in_xbc")
            dt = _mm(h, ssm_w_in_t, j, mode="mask_cols", col_offs=(2 * d_inner + 2 * gn,), n_cols=n_heads,
                     tn=LANES, tm=tm, rc=rc_t, seq_len=seq_len, transposed=True, name="ssm_in_dt")
            yn = _ssd(xbc, z, dt, dt_bias, a_log, d_skip, ssm_norm_g, j, batch, seq_len, tl, SSD_GROUPS_PER_STEP)
            xf = _mm(yn, ssm_w_out, j, mode="res", col_offs=(0,), n_cols=d, tn=tn2, tm=tm_s, rc=tm_s,
                     seq_len=seq_len, res=xf, name="ssm_out")
        h = _rmsnorm(xf, norm_ffn_g, i, BF16, tnorm)
        act = _mm(h, ffn_w_up, i, mode="ffn", col_offs=(0, f), n_cols=f, tn=tn, tm=tm2, rc=rc, seq_len=seq_len,
                  out_dtype=BF16, conv_w=ffn_w_dw, conv_b=ffn_b_dw, name="ffn_up_conv_gate")
        xf = _mm(act, ffn_w_down, i, mode="res", col_offs=(0,), n_cols=d, tn=tn, tm=tm_k, rc=rc, seq_len=seq_len,
                 res=xf, name="ffn_down")
    out = _rmsnorm(xf, norm_final_g, 0, F32, tnorm)
    return out.reshape(batch, seq_len, d)
```

```python
import functools

import jax
import jax.numpy as jnp
from jax import lax
from jax.experimental import pallas as pl
from jax.experimental.pallas import tpu as pltpu

RMS_EPS = 1e-6
LN_EPS = 1e-5

SSD_CHUNK = 64
SSD_HEAD_DIM = 64
SSD_N_GROUPS = 8
SSD_D_STATE = 128
SSD_GROUPS_PER_STEP = 4
Q_SHIFT = SSD_CHUNK.bit_length() - 1
HEAD_SHIFT = SSD_HEAD_DIM.bit_length() - 1

LANES = 128
SUBLANES = 8
MXU_DIM = 256
VMEM_LIMIT_BYTES = 56 * 1024 * 1024

CONV_HALO = 32
NEG_BIG = -1e30
LOG2_E = 1.4426950408889634

F32 = jnp.float32
BF16 = jnp.bfloat16


def _params(semantics):
    return pltpu.CompilerParams(dimension_semantics=semantics, vmem_limit_bytes=VMEM_LIMIT_BYTES)


def _rmsnorm_kernel(x_ref, g_ref, o_ref):
    x = x_ref[...]
    ms = jnp.mean(x * x, axis=-1, keepdims=True)
    o_ref[...] = (x * lax.rsqrt(ms + RMS_EPS) * g_ref[...]).astype(o_ref.dtype)


def _rmsnorm(x, gains, layer, out_dtype, tm):
    t, d = x.shape
    return pl.pallas_call(
        _rmsnorm_kernel,
        out_shape=jax.ShapeDtypeStruct((t, d), out_dtype),
        grid=(t // tm,),
        in_specs=[pl.BlockSpec((tm, d), lambda m: (m, 0)),
                  pl.BlockSpec((None, 1, d), lambda m: (layer, 0, 0))],
        out_specs=pl.BlockSpec((tm, d), lambda m: (m, 0)),
        compiler_params=_params(("parallel",)),
        name="rmsnorm",
    )(x, gains)


def _tile_rows(x8, rows):
    return jnp.tile(x8, (rows // SUBLANES, 1))


def _causal_conv_rows(u, tail, p8_ref, kc):
    rc = u.shape[0]
    ext = jnp.concatenate([tail, u], axis=0)
    out = u * _tile_rows(p8_ref[kc - 1], rc) + _tile_rows(p8_ref[kc], rc)
    for k in range(kc - 1):
        start = SUBLANES - (kc - 1 - k)
        out = out + ext[start:start + rc, :] * _tile_rows(p8_ref[k], rc)
    return out


def _mm_kernel(*refs, mode, nw, tiles_per_seq, n_valid, rc, transposed, norm):
    a_ref = refs[0]
    refs = refs[1:]
    if norm:
        gain_ref, rstd_ref = refs[0], refs[-1]
        refs = refs[1:-1]
    w_refs = refs[:nw]
    rest = refs[nw:]
    m = pl.program_id(1)
    tm = a_ref.shape[0]
    tn = w_refs[0].shape[0] if transposed else w_refs[0].shape[1]
    conv_refs, p8_refs, carry_ref = (), (), None

    if mode == "glu":
        ba_ref, bg_ref, o_ref, wb_ref = rest
    elif mode == "res":
        res_ref, o_ref, wb_ref = rest
    elif mode == "res_bias":
        res_ref, b_ref, o_ref, wb_ref = rest
    elif mode == "plain" or mode == "mask_cols":
        o_ref, wb_ref = rest
    elif mode == "conv_silu":
        cw_ref, cb_ref, o_ref, wb_ref, p8, carry_ref = rest
        conv_refs, p8_refs = [(cw_ref, cb_ref)], [p8]
    elif mode == "ffn":
        cwg_ref, cwv_ref, cbg_ref, cbv_ref, o_ref, wb_ref, p8g, p8v, carry_ref = rest
        conv_refs, p8_refs = [(cwg_ref, cbg_ref), (cwv_ref, cbv_ref)], [p8g, p8v]
    else:
        raise ValueError(mode)
    kc = conv_refs[0][0].shape[0] if conv_refs else 0

    @pl.when(m == 0)
    def _():
        for i, w_ref in enumerate(w_refs):
            w = w_ref[...] * gain_ref[...] if norm else w_ref[...]
            wb_ref[:, i * tn:(i + 1) * tn] = (w.T if transposed else w).astype(BF16)
        for (cw_ref, cb_ref), p8_ref in zip(conv_refs, p8_refs):
            for k in range(kc):
                p8_ref[k] = jnp.broadcast_to(cw_ref[k:k + 1, :], p8_ref.shape[1:])
            p8_ref[kc] = jnp.broadcast_to(cb_ref[...], p8_ref.shape[1:])

    if conv_refs:
        @pl.when((m % tiles_per_seq) == 0)
        def _():
            carry_ref[...] = jnp.zeros(carry_ref.shape, F32)

    if norm:
        @pl.when(pl.program_id(0) == 0)
        def _():
            x = a_ref[...]
            ms = jnp.mean(x * x, axis=-1, keepdims=True)
            rstd_ref[m] = jnp.broadcast_to(lax.rsqrt(ms + RMS_EPS), rstd_ref.shape[1:])

    def chunk_matmuls(c):
        rows = slice(c * rc, (c + 1) * rc)
        a = a_ref[rows, :].astype(BF16)
        acc = jnp.dot(a, wb_ref[...], preferred_element_type=F32)
        accs = [acc[:, i * tn:(i + 1) * tn] for i in range(nw)]
        if norm:
            scale = jnp.tile(rstd_ref[m, rows, :], (1, accs[0].shape[1] // LANES))
            accs = [acc * scale for acc in accs]
        return accs

    tails = [carry_ref[i] for i in range(len(conv_refs))]
    n_chunks = tm // rc
    accs_next = chunk_matmuls(0)
    for c in range(n_chunks):
        rows = slice(c * rc, (c + 1) * rc)
        accs = accs_next
        if c + 1 < n_chunks:
            accs_next = chunk_matmuls(c + 1)
        if mode == "glu":
            o_ref[rows, :] = (accs[0] + ba_ref[...]) * jax.nn.sigmoid(accs[1] + bg_ref[...])
        elif mode == "res":
            o_ref[rows, :] = res_ref[rows, :] + accs[0]
        elif mode == "res_bias":
            o_ref[rows, :] = res_ref[rows, :] + (accs[0] + b_ref[...])
        elif mode == "plain":
            o_ref[rows, :] = accs[0].astype(o_ref.dtype)
        elif mode == "mask_cols":
            lane = lax.broadcasted_iota(jnp.int32, accs[0].shape, 1)
            o_ref[rows, :] = jnp.where(lane < n_valid, accs[0], 0.0)
        else:
            convs = [_causal_conv_rows(u, tail, p8_ref, kc) for u, tail, p8_ref in zip(accs, tails, p8_refs)]
            tails = [u[rc - SUBLANES:rc, :] for u in accs]
            if mode == "conv_silu":
                o_ref[rows, :] = convs[0] * jax.nn.sigmoid(convs[0])
            else:
                o_ref[rows, :] = (convs[0] * jax.nn.sigmoid(convs[0]) * convs[1]).astype(o_ref.dtype)
    for i, tail in enumerate(tails):
        carry_ref[i] = tail


def _mm(a, w, layer, *, mode, col_offs, n_cols, tn, tm, rc, seq_len, out_dtype=F32, transposed=False,
        gain=None, gain_layer=None, bias=None, conv_w=None, conv_b=None, res=None, name):
    t, k = a.shape
    nw = len(col_offs)
    norm = gain is not None
    assert not (norm and transposed)
    assert t % tm == 0 and seq_len % tm == 0 and tm % rc == 0 and all(off % tn == 0 for off in col_offs)
    n_tiles = pl.cdiv(n_cols, tn)
    out_cols = n_tiles * tn
    offs = [off // tn for off in col_offs]

    def col_spec(rows, ob):
        return pl.BlockSpec((None, rows, tn), lambda n, m: (layer, 0, n + ob))

    def weight_spec(ob):
        if transposed:
            return pl.BlockSpec((None, tn, k), lambda n, m: (layer, n + ob, 0))
        return col_spec(k, ob)

    in_specs = [pl.BlockSpec((tm, k), lambda n, m: (m, 0))]
    args = [a]
    if norm:
        in_specs.append(pl.BlockSpec((None, k, 1), lambda n, m: (gain_layer, 0, 0)))
        args.append(gain)
    for ob in offs:
        in_specs.append(weight_spec(ob))
        args.append(w)
    tile_spec = pl.BlockSpec((tm, tn), lambda n, m: (m, n))
    scratch = [pltpu.VMEM((k, nw * tn), BF16)]

    if mode == "glu":
        in_specs += [col_spec(1, ob) for ob in offs]
        args += [bias, bias]
    elif mode == "res":
        in_specs.append(tile_spec)
        args.append(res)
    elif mode == "res_bias":
        in_specs += [tile_spec, col_spec(1, offs[0])]
        args += [res, bias]
    elif mode == "conv_silu":
        kc = conv_w.shape[1]
        in_specs += [col_spec(kc, 0), col_spec(1, 0)]
        args += [conv_w, conv_b]
    elif mode == "ffn":
        kc = conv_w.shape[1]
        in_specs += [col_spec(kc, ob) for ob in offs] + [col_spec(1, ob) for ob in offs]
        args += [conv_w, conv_w, conv_b, conv_b]
    if mode in ("conv_silu", "ffn"):
        assert kc - 1 <= SUBLANES
        scratch += [pltpu.VMEM((kc + 1, SUBLANES, tn), F32) for _ in offs]
        scratch += [pltpu.VMEM((nw, SUBLANES, tn), F32)]
    if norm:
        scratch += [pltpu.VMEM((t // tm, tm, LANES), F32)]

    kern = functools.partial(_mm_kernel, mode=mode, nw=nw, tiles_per_seq=seq_len // tm, n_valid=n_cols,
                             rc=rc, transposed=transposed, norm=norm)
    return pl.pallas_call(
        kern,
        out_shape=jax.ShapeDtypeStruct((t, out_cols), out_dtype),
        grid=(n_tiles, t // tm),
        in_specs=in_specs,
        out_specs=tile_spec,
        scratch_shapes=scratch,
        compiler_params=_params(("arbitrary" if norm else "parallel", "arbitrary")),
        name=name,
    )(*args)


def _conf_mid_kernel(cur_ref, halo_ref, w_ref, b_ref, g_ref, beta_ref, o_ref, ext_ref, conv_ref, w8_ref,
                     *, tm, tiles_per_seq, rc):
    kc = w_ref.shape[0]
    d = cur_ref.shape[1]
    seq_start = (pl.program_id(0) % tiles_per_seq) == 0

    @pl.when(seq_start)
    def _():
        ext_ref[0:CONV_HALO, :] = jnp.zeros((CONV_HALO, d), F32)

    @pl.when(jnp.logical_not(seq_start))
    def _():
        ext_ref[0:CONV_HALO, :] = halo_ref[...]

    ext_ref[CONV_HALO:CONV_HALO + tm, :] = cur_ref[...]
    for k in range(kc):
        w8_ref[k] = jnp.broadcast_to(w_ref[k:k + 1, :], (SUBLANES, d))

    base = CONV_HALO - (kc - 1)
    for cj in range(d // LANES):
        cols = slice(cj * LANES, (cj + 1) * LANES)

        def row_body(ri, carry, cols=cols):
            r0 = pl.multiple_of(ri * rc, rc)
            e = ext_ref[pl.ds(r0, rc + CONV_HALO), cols]
            acc = jnp.zeros((rc, LANES), F32) + b_ref[:, cols]
            for r in range(SUBLANES):
                taps = [k for k in range(kc) if (base + k) % SUBLANES == r]
                if not taps:
                    continue
                n = rc if r == 0 else rc + SUBLANES
                part = None
                for k in taps:
                    lo = (base + k) // SUBLANES * SUBLANES
                    term = e[lo:lo + n, :] * _tile_rows(w8_ref[k, :, cols], n)
                    part = term if part is None else part + term
                acc = acc + part[r:r + rc, :]
            conv_ref[pl.ds(r0, rc), cols] = acc
            return carry

        lax.fori_loop(0, tm // rc, row_body, 0)

    c = conv_ref[...]
    mu = jnp.mean(c, axis=-1, keepdims=True)
    xc = c - mu
    var = jnp.mean(xc * xc, axis=-1, keepdims=True)
    y = xc * lax.rsqrt(var + LN_EPS) * g_ref[...] + beta_ref[...]
    o_ref[...] = (y * jax.nn.sigmoid(y)).astype(o_ref.dtype)


def _conf_mid(v, w_dw, b_dw, ln_g, ln_b, layer, seq_len, tm):
    t, d = v.shape
    kc = w_dw.shape[1]
    assert kc - 1 <= CONV_HALO and seq_len % tm == 0 and tm % CONV_HALO == 0 and d % LANES == 0
    rc = min(64, tm)
    halo_per_tile = tm // CONV_HALO
    row = lambda m: (layer, 0, 0)
    kern = functools.partial(_conf_mid_kernel, tm=tm, tiles_per_seq=seq_len // tm, rc=rc)
    return pl.pallas_call(
        kern,
        out_shape=jax.ShapeDtypeStruct((t, d), BF16),
        grid=(t // tm,),
        in_specs=[pl.BlockSpec((tm, d), lambda m: (m, 0)),
                  pl.BlockSpec((CONV_HALO, d), lambda m: (jnp.maximum(m * halo_per_tile - 1, 0), 0)),
                  pl.BlockSpec((None, kc, d), row),
                  pl.BlockSpec((None, 1, d), row),
                  pl.BlockSpec((None, 1, d), row),
                  pl.BlockSpec((None, 1, d), row)],
        out_specs=pl.BlockSpec((tm, d), lambda m: (m, 0)),
        scratch_shapes=[pltpu.VMEM((CONV_HALO + tm, d), F32), pltpu.VMEM((tm, d), F32),
                        pltpu.VMEM((kc, SUBLANES, d), F32)],
        compiler_params=_params(("parallel",)),
        name="conformer_conv_ln_swish",
    )(v, v, w_dw, b_dw, ln_g, ln_b)


def _split2(x):
    hi = x.astype(BF16)
    lo = (x - hi.astype(F32)).astype(BF16)
    return hi, lo


def _split3(x):
    hi, lo = _split2(x)
    lo2 = (x - hi.astype(F32) - lo.astype(F32)).astype(BF16)
    return hi, lo, lo2


def _ssd_kernel(x_ref, b_ref, c_ref, z_ref, dt_ref, dtb_ref, alog_ref, dskip_ref, ng_ref, o_ref,
                state_ref, parts_ref, dte_ref, acse_ref, *, tl, gw, gs):
    q = SSD_CHUNK
    lt = pl.program_id(1)
    g0 = pl.program_id(2) * gs
    hpg = gw // SSD_HEAD_DIM

    @pl.when(lt == 0)
    def _():
        for gi in range(gs):
            state_ref[g0 + gi] = jnp.zeros(state_ref.shape[1:], F32)

    @pl.when(g0 == 0)
    def _():
        pre = dt_ref[...] + dtb_ref[...]
        dt = jnp.maximum(pre, 0.0) + jnp.log1p(jnp.exp(-jnp.abs(pre)))
        a = dt * (-jnp.exp(alog_ref[...]) * LOG2_E)
        r = lax.broadcasted_iota(jnp.int32, (tl, tl), 0)
        c = lax.broadcasted_iota(jnp.int32, (tl, tl), 1)
        tri = jnp.where((r >= c) & ((r >> Q_SHIFT) == (c >> Q_SHIFT)), 1.0, 0.0).astype(BF16)
        acs = jnp.zeros((tl, LANES), F32)
        for part in _split3(a):
            acs = acs + jnp.dot(tri, part, preferred_element_type=F32)
        for i, part in enumerate(_split2(dt) + _split2(acs)):
            parts_ref[i * tl:(i + 1) * tl, :] = part

    hrow = lax.broadcasted_iota(jnp.int32, (LANES, gs * gw), 0)
    hcol = lax.broadcasted_iota(jnp.int32, (LANES, gs * gw), 1)
    expand = jnp.where(hrow == g0 * hpg + (hcol >> HEAD_SHIFT), 1.0, 0.0).astype(BF16)
    ex = jnp.dot(parts_ref[...], expand, preferred_element_type=F32)
    dte_ref[...] = ex[0:tl] + ex[tl:2 * tl]
    acse_ref[...] = ex[2 * tl:3 * tl] + ex[3 * tl:4 * tl]

    row = lax.broadcasted_iota(jnp.int32, (q, gw), 0)
    pos = lax.broadcasted_iota(jnp.int32, (q, gw), 1) & (SSD_HEAD_DIM - 1)
    causal = row >= pos
    diagonal = row == pos
    br = lax.broadcasted_iota(jnp.int32, (MXU_DIM, MXU_DIM), 0) >> HEAD_SHIFT
    bc = lax.broadcasted_iota(jnp.int32, (MXU_DIM, MXU_DIM), 1) >> HEAD_SHIFT
    head_mask = jnp.where(br == bc, 1.0, 0.0).astype(BF16)
    heads_per_dot = MXU_DIM // SSD_HEAD_DIM

    for ci in range(tl // q):
        rows = pl.ds(ci * q, q)
        for gi in range(gs):
            lanes = slice(gi * gw, (gi + 1) * gw)
            nlanes = slice(gi * SSD_D_STATE, (gi + 1) * SSD_D_STATE)
            x = x_ref[rows, lanes]
            bm = b_ref[rows, nlanes].astype(BF16)
            cm = c_ref[rows, nlanes].astype(BF16)
            acs = acse_ref[rows, lanes]
            a_last = acse_ref[pl.ds(ci * q + q - 1, 1), lanes]
            xd = x * dte_ref[rows, lanes]
            xd_bf = xd.astype(BF16)

            prev = state_ref[g0 + gi]
            y = jnp.dot(cm, prev.astype(BF16), preferred_element_type=F32) * jnp.exp2(acs)

            wgt = (jnp.exp2(a_last - acs) * xd).astype(BF16)
            s_new = lax.dot_general(bm, wgt, (((0,), (0,)), ((), ())), preferred_element_type=F32)
            state_ref[g0 + gi] = prev * jnp.exp2(a_last) + s_new

            cb = lax.dot_general(cm, jnp.concatenate([bm] * hpg, axis=0), (((1,), (1,)), ((), ())),
                                 preferred_element_type=F32)
            acs_s = jnp.sum(jnp.where(diagonal, acs, 0.0), axis=0, keepdims=True)
            decay = jnp.exp2(jnp.where(causal, acs - acs_s, NEG_BIG))
            mix = (cb * decay).astype(BF16)
            parts = []
            for j in range(gw // MXU_DIM):
                cols = slice(j * MXU_DIM, (j + 1) * MXU_DIM)
                blockdiag = jnp.concatenate([xd_bf[:, cols]] * heads_per_dot, axis=0) * head_mask
                parts.append(jnp.dot(mix[:, cols], blockdiag, preferred_element_type=F32))
            y = y + jnp.concatenate(parts, axis=1) + dskip_ref[:, lanes] * x

            z = z_ref[rows, lanes]
            yg = y * (z * jax.nn.sigmoid(z))
            ms = jnp.mean(yg * yg, axis=-1, keepdims=True)
            o_ref[rows, lanes] = (yg * lax.rsqrt(ms + RMS_EPS) * ng_ref[:, lanes]).astype(o_ref.dtype)


def _ssd(xbc, z, dt, dt_bias, a_log, d_skip, norm_g, layer, batch, seq_len, tl, gs):
    t, d_inner = z.shape
    gw = d_inner // SSD_N_GROUPS
    bw = gs * SSD_D_STATE
    assert gw % MXU_DIM == 0 and SSD_D_STATE == LANES and seq_len % tl == 0 and tl % SSD_CHUNK == 0
    assert SSD_N_GROUPS % gs == 0 and d_inner % bw == 0
    nl = seq_len // tl
    b_blk = d_inner // bw
    c_blk = b_blk + SSD_N_GROUPS // gs
    rows = lambda b, l, g: (b * nl + l, g)
    head_row = lambda b, l, g: (layer, 0, 0)
    group_row = lambda b, l, g: (layer, 0, g)
    kern = functools.partial(_ssd_kernel, tl=tl, gw=gw, gs=gs)
    return pl.pallas_call(
        kern,
        out_shape=jax.ShapeDtypeStruct((t, d_inner), BF16),
        grid=(batch, nl, SSD_N_GROUPS // gs),
        in_specs=[pl.BlockSpec((tl, gs * gw), rows),
                  pl.BlockSpec((tl, bw), lambda b, l, g: (b * nl + l, b_blk + g)),
                  pl.BlockSpec((tl, bw), lambda b, l, g: (b * nl + l, c_blk + g)),
                  pl.BlockSpec((tl, gs * gw), rows),
                  pl.BlockSpec((tl, LANES), lambda b, l, g: (b * nl + l, 0)),
                  pl.BlockSpec((None, 1, LANES), head_row),
                  pl.BlockSpec((None, 1, LANES), head_row),
                  pl.BlockSpec((None, 1, gs * gw), group_row),
                  pl.BlockSpec((None, 1, gs * gw), group_row)],
        out_specs=pl.BlockSpec((tl, gs * gw), rows),
        scratch_shapes=[pltpu.VMEM((SSD_N_GROUPS, SSD_D_STATE, gw), F32),
                        pltpu.VMEM((4 * tl, LANES), BF16),
                        pltpu.VMEM((tl, gs * gw), F32),
                        pltpu.VMEM((tl, gs * gw), F32)],
        compiler_params=_params(("parallel", "arbitrary", "arbitrary")),
        name="ssd_scan_gate_norm",
    )(xbc, xbc, xbc, z, dt, dt_bias, a_log, d_skip, norm_g)


def _row3(p):
    return p.reshape(p.shape[0], 1, p.shape[1])


def _pad_lanes(p):
    return jnp.pad(p, ((0, 0), (0, LANES - p.shape[1])))


def kernel(x, norm_mix_g, norm_ffn_g, norm_final_g, cv_w_in, cv_b_in, cv_w_dw, cv_b_dw, cv_ln_g, cv_ln_b, cv_w_out, cv_b_out, ssm_w_in, ssm_w_conv, ssm_b_conv, ssm_dt_bias, ssm_a_log, ssm_d, ssm_norm_g, ssm_w_out, ffn_w_up, ffn_w_dw, ffn_b_dw, ffn_w_down):
    batch, seq_len, d = x.shape
    t = batch * seq_len
    depth = norm_mix_g.shape[0]
    d_inner = ssm_w_out.shape[1]
    n_heads = ssm_dt_bias.shape[1]
    gn = SSD_N_GROUPS * SSD_D_STATE
    f = ffn_w_down.shape[1]
    assert n_heads * SSD_HEAD_DIM == d_inner and n_heads <= LANES
    assert ssm_w_in.shape[2] == 2 * d_inner + 2 * gn + n_heads

    tm = min(1024, seq_len)
    tm2 = min(2048, seq_len)
    tm_k = min(512, seq_len)
    tm_s = min(256, seq_len)
    tn = min(512, d)
    tn2 = min(1024, d)
    tnorm = min(512, seq_len)
    tconv = min(256, seq_len)
    tl = min(512, seq_len)
    rc = min(256, seq_len)
    rc_t = min(512, seq_len)

    mix_gain = norm_mix_g.reshape(depth, d, 1)
    ffn_gain = norm_ffn_g.reshape(depth, d, 1)
    norm_mix_g = _row3(norm_mix_g)
    norm_final_g = norm_final_g.reshape(1, 1, d)
    cv_b_in, cv_b_dw, cv_ln_g, cv_ln_b, cv_b_out = map(_row3, (cv_b_in, cv_b_dw, cv_ln_g, cv_ln_b, cv_b_out))
    ssm_b_conv, ssm_norm_g, ffn_b_dw = map(_row3, (ssm_b_conv, ssm_norm_g, ffn_b_dw))
    dt_bias = _row3(_pad_lanes(ssm_dt_bias))
    a_log = _row3(_pad_lanes(ssm_a_log))
    d_skip = _row3(jnp.repeat(ssm_d, SSD_HEAD_DIM, axis=1))
    ssm_w_in_t = jnp.swapaxes(ssm_w_in, 1, 2)

    xf = x.reshape(t, d)
    for i in range(depth):
        j = i // 2
        if i % 2 == 0:
            v = _mm(xf, cv_w_in, j, mode="glu", col_offs=(0, d), n_cols=d, tn=tn, tm=tm, rc=rc, seq_len=seq_len,
                    gain=mix_gain, gain_layer=i, bias=cv_b_in, name="conformer_in_glu")
            s = _conf_mid(v, cv_w_dw, cv_b_dw, cv_ln_g, cv_ln_b, j, seq_len, tconv)
            xf = _mm(s, cv_w_out, j, mode="res_bias", col_offs=(0,), n_cols=d, tn=tn2, tm=tm, rc=rc,
                     seq_len=seq_len, res=xf, bias=cv_b_out, name="conformer_out")
        else:
            h = _rmsnorm(xf, norm_mix_g, i, BF16, tnorm)
            z = _mm(h, ssm_w_in_t, j, mode="plain", col_offs=(0,), n_cols=d_inner, tn=tn2, tm=tm, rc=rc_t,
                    seq_len=seq_len, transposed=True, name="ssm_in_z")
            xbc = _mm(h, ssm_w_in_t, j, mode="conv_silu", col_offs=(d_inner,), n_cols=d_inner + 2 * gn, tn=tn2,
                      tm=tm, rc=rc, seq_len=seq_len, transposed=True, conv_w=ssm_w_conv, conv_b=ssm_b_conv,
                      name="ssm_in_xbc")
            dt = _mm(h, ssm_w_in_t, j, mode="mask_cols", col_offs=(2 * d_inner + 2 * gn,), n_cols=n_heads,
                     tn=LANES, tm=tm, rc=rc_t, seq_len=seq_len, transposed=True, name="ssm_in_dt")
            yn = _ssd(xbc, z, dt, dt_bias, a_log, d_skip, ssm_norm_g, j, batch, seq_len, tl, SSD_GROUPS_PER_STEP)
            xf = _mm(yn, ssm_w_out, j, mode="res", col_offs=(0,), n_cols=d, tn=tn2, tm=tm_s, rc=tm_s,
                     seq_len=seq_len, res=xf, name="ssm_out")
        act = _mm(xf, ffn_w_up, i, mode="ffn", col_offs=(0, f), n_cols=f, tn=tn, tm=tm, rc=rc, seq_len=seq_len,
                  out_dtype=BF16, gain=ffn_gain, gain_layer=i, conv_w=ffn_w_dw, conv_b=ffn_b_dw,
                  name="ffn_up_conv_gate")
        xf = _mm(act, ffn_w_down, i, mode="res", col_offs=(0,), n_cols=d, tn=tn, tm=tm_k, rc=rc, seq_len=seq_len,
                 res=xf, name="ffn_down")
    out = _rmsnorm(xf, norm_final_g, 0, F32, tnorm)
    return out.reshape(batch, seq_len, d)
```

```python
import functools

import jax
import jax.numpy as jnp
from jax import lax
from jax.experimental import pallas as pl
from jax.experimental.pallas import tpu as pltpu

RMS_EPS = 1e-6
LN_EPS = 1e-5

SSD_CHUNK = 64
SSD_HEAD_DIM = 64
SSD_N_GROUPS = 8
SSD_D_STATE = 128
SSD_GROUPS_PER_STEP = 4
Q_SHIFT = SSD_CHUNK.bit_length() - 1
HEAD_SHIFT = SSD_HEAD_DIM.bit_length() - 1

LANES = 128
SUBLANES = 8
MXU_DIM = 256
VMEM_LIMIT_BYTES = 56 * 1024 * 1024

CONV_HALO = 32
NEG_BIG = -1e30
LOG2_E = 1.4426950408889634

F32 = jnp.float32
BF16 = jnp.bfloat16


def _params(semantics):
    return pltpu.CompilerParams(dimension_semantics=semantics, vmem_limit_bytes=VMEM_LIMIT_BYTES)


def _rmsnorm_kernel(x_ref, g_ref, o_ref):
    x = x_ref[...]
    ms = jnp.mean(x * x, axis=-1, keepdims=True)
    o_ref[...] = (x * lax.rsqrt(ms + RMS_EPS) * g_ref[...]).astype(o_ref.dtype)


def _rmsnorm(x, gains, layer, out_dtype, tm):
    t, d = x.shape
    return pl.pallas_call(
        _rmsnorm_kernel,
        out_shape=jax.ShapeDtypeStruct((t, d), out_dtype),
        grid=(t // tm,),
        in_specs=[pl.BlockSpec((tm, d), lambda m: (m, 0)),
                  pl.BlockSpec((None, 1, d), lambda m: (layer, 0, 0))],
        out_specs=pl.BlockSpec((tm, d), lambda m: (m, 0)),
        compiler_params=_params(("parallel",)),
        name="rmsnorm",
    )(x, gains)


def _tile_rows(x8, rows):
    return jnp.tile(x8, (rows // SUBLANES, 1))


def _causal_conv_rows(u, tail, p8_ref, kc):
    rc = u.shape[0]
    ext = jnp.concatenate([tail, u], axis=0)
    out = u * _tile_rows(p8_ref[kc - 1], rc) + _tile_rows(p8_ref[kc], rc)
    for k in range(kc - 1):
        start = SUBLANES - (kc - 1 - k)
        out = out + ext[start:start + rc, :] * _tile_rows(p8_ref[k], rc)
    return out


def _mm_kernel(*refs, mode, nw, tiles_per_seq, n_valid, rc, transposed, norm):
    a_ref = refs[0]
    refs = refs[1:]
    if norm:
        gain_ref, rstd_ref = refs[0], refs[-1]
        refs = refs[1:-1]
    w_refs = refs[:nw]
    rest = refs[nw:]
    m = pl.program_id(1)
    tm = a_ref.shape[0]
    conv_refs, p8_refs, carry_ref = (), (), None

    if mode == "glu":
        ba_ref, bg_ref, o_ref, *wb_refs = rest
    elif mode == "res":
        res_ref, o_ref, *wb_refs = rest
    elif mode == "res_bias":
        res_ref, b_ref, o_ref, *wb_refs = rest
    elif mode == "plain" or mode == "mask_cols":
        o_ref, *wb_refs = rest
    elif mode == "conv_silu":
        cw_ref, cb_ref, o_ref, wb0, p8, carry_ref = rest
        wb_refs, conv_refs, p8_refs = [wb0], [(cw_ref, cb_ref)], [p8]
    elif mode == "ffn":
        cwg_ref, cwv_ref, cbg_ref, cbv_ref, o_ref, wb0, wb1, p8g, p8v, carry_ref = rest
        wb_refs, conv_refs, p8_refs = [wb0, wb1], [(cwg_ref, cbg_ref), (cwv_ref, cbv_ref)], [p8g, p8v]
    else:
        raise ValueError(mode)
    kc = conv_refs[0][0].shape[0] if conv_refs else 0

    @pl.when(m == 0)
    def _():
        for w_ref, wb_ref in zip(w_refs, wb_refs):
            w = w_ref[...] * gain_ref[...] if norm else w_ref[...]
            wb_ref[...] = (w.T if transposed else w).astype(BF16)
        for (cw_ref, cb_ref), p8_ref in zip(conv_refs, p8_refs):
            for k in range(kc):
                p8_ref[k] = jnp.broadcast_to(cw_ref[k:k + 1, :], p8_ref.shape[1:])
            p8_ref[kc] = jnp.broadcast_to(cb_ref[...], p8_ref.shape[1:])

    if conv_refs:
        @pl.when((m % tiles_per_seq) == 0)
        def _():
            carry_ref[...] = jnp.zeros(carry_ref.shape, F32)

    if norm:
        @pl.when(pl.program_id(0) == 0)
        def _():
            x = a_ref[...]
            ms = jnp.mean(x * x, axis=-1, keepdims=True)
            rstd_ref[m] = jnp.broadcast_to(lax.rsqrt(ms + RMS_EPS), rstd_ref.shape[1:])

    def chunk_matmuls(c):
        rows = slice(c * rc, (c + 1) * rc)
        a = a_ref[rows, :].astype(BF16)
        accs = [jnp.dot(a, wb_ref[...], preferred_element_type=F32) for wb_ref in wb_refs]
        if norm:
            scale = jnp.tile(rstd_ref[m, rows, :], (1, accs[0].shape[1] // LANES))
            accs = [acc * scale for acc in accs]
        return accs

    tails = [carry_ref[i] for i in range(len(conv_refs))]
    n_chunks = tm // rc
    accs_next = chunk_matmuls(0)
    for c in range(n_chunks):
        rows = slice(c * rc, (c + 1) * rc)
        accs = accs_next
        if c + 1 < n_chunks:
            accs_next = chunk_matmuls(c + 1)
        if mode == "glu":
            o_ref[rows, :] = (accs[0] + ba_ref[...]) * jax.nn.sigmoid(accs[1] + bg_ref[...])
        elif mode == "res":
            o_ref[rows, :] = res_ref[rows, :] + accs[0]
        elif mode == "res_bias":
            o_ref[rows, :] = res_ref[rows, :] + (accs[0] + b_ref[...])
        elif mode == "plain":
            o_ref[rows, :] = accs[0].astype(o_ref.dtype)
        elif mode == "mask_cols":
            lane = lax.broadcasted_iota(jnp.int32, accs[0].shape, 1)
            o_ref[rows, :] = jnp.where(lane < n_valid, accs[0], 0.0)
        else:
            convs = [_causal_conv_rows(u, tail, p8_ref, kc) for u, tail, p8_ref in zip(accs, tails, p8_refs)]
            tails = [u[rc - SUBLANES:rc, :] for u in accs]
            if mode == "conv_silu":
                o_ref[rows, :] = convs[0] * jax.nn.sigmoid(convs[0])
            else:
                o_ref[rows, :] = (convs[0] * jax.nn.sigmoid(convs[0]) * convs[1]).astype(o_ref.dtype)
    for i, tail in enumerate(tails):
        carry_ref[i] = tail


def _mm(a, w, layer, *, mode, col_offs, n_cols, tn, tm, rc, seq_len, out_dtype=F32, transposed=False,
        gain=None, gain_layer=None, bias=None, conv_w=None, conv_b=None, res=None, name):
    t, k = a.shape
    nw = len(col_offs)
    norm = gain is not None
    assert not (norm and transposed)
    assert t % tm == 0 and seq_len % tm == 0 and tm % rc == 0 and all(off % tn == 0 for off in col_offs)
    n_tiles = pl.cdiv(n_cols, tn)
    out_cols = n_tiles * tn
    offs = [off // tn for off in col_offs]

    def col_spec(rows, ob):
        return pl.BlockSpec((None, rows, tn), lambda n, m: (layer, 0, n + ob))

    def weight_spec(ob):
        if transposed:
            return pl.BlockSpec((None, tn, k), lambda n, m: (layer, n + ob, 0))
        return col_spec(k, ob)

    in_specs = [pl.BlockSpec((tm, k), lambda n, m: (m, 0))]
    args = [a]
    if norm:
        in_specs.append(pl.BlockSpec((None, k, 1), lambda n, m: (gain_layer, 0, 0)))
        args.append(gain)
    for ob in offs:
        in_specs.append(weight_spec(ob))
        args.append(w)
    tile_spec = pl.BlockSpec((tm, tn), lambda n, m: (m, n))
    scratch = [pltpu.VMEM((k, tn), BF16) for _ in offs]

    if mode == "glu":
        in_specs += [col_spec(1, ob) for ob in offs]
        args += [bias, bias]
    elif mode == "res":
        in_specs.append(tile_spec)
        args.append(res)
    elif mode == "res_bias":
        in_specs += [tile_spec, col_spec(1, offs[0])]
        args += [res, bias]
    elif mode == "conv_silu":
        kc = conv_w.shape[1]
        in_specs += [col_spec(kc, 0), col_spec(1, 0)]
        args += [conv_w, conv_b]
    elif mode == "ffn":
        kc = conv_w.shape[1]
        in_specs += [col_spec(kc, ob) for ob in offs] + [col_spec(1, ob) for ob in offs]
        args += [conv_w, conv_w, conv_b, conv_b]
    if mode in ("conv_silu", "ffn"):
        assert kc - 1 <= SUBLANES
        scratch += [pltpu.VMEM((kc + 1, SUBLANES, tn), F32) for _ in offs]
        scratch += [pltpu.VMEM((nw, SUBLANES, tn), F32)]
    if norm:
        scratch += [pltpu.VMEM((t // tm, tm, LANES), F32)]

    kern = functools.partial(_mm_kernel, mode=mode, nw=nw, tiles_per_seq=seq_len // tm, n_valid=n_cols,
                             rc=rc, transposed=transposed, norm=norm)
    return pl.pallas_call(
        kern,
        out_shape=jax.ShapeDtypeStruct((t, out_cols), out_dtype),
        grid=(n_tiles, t // tm),
        in_specs=in_specs,
        out_specs=tile_spec,
        scratch_shapes=scratch,
        compiler_params=_params(("arbitrary" if norm else "parallel", "arbitrary")),
        name=name,
    )(*args)


def _conf_mid_kernel(cur_ref, halo_ref, w_ref, b_ref, g_ref, beta_ref, o_ref, ext_ref, conv_ref, w8_ref,
                     *, tm, tiles_per_seq, rc):
    kc = w_ref.shape[0]
    d = cur_ref.shape[1]
    seq_start = (pl.program_id(0) % tiles_per_seq) == 0

    @pl.when(seq_start)
    def _():
        ext_ref[0:CONV_HALO, :] = jnp.zeros((CONV_HALO, d), F32)

    @pl.when(jnp.logical_not(seq_start))
    def _():
        ext_ref[0:CONV_HALO, :] = halo_ref[...]

    ext_ref[CONV_HALO:CONV_HALO + tm, :] = cur_ref[...]
    for k in range(kc):
        w8_ref[k] = jnp.broadcast_to(w_ref[k:k + 1, :], (SUBLANES, d))

    base = CONV_HALO - (kc - 1)
    for cj in range(d // LANES):
        cols = slice(cj * LANES, (cj + 1) * LANES)

        def row_body(ri, carry, cols=cols):
            r0 = pl.multiple_of(ri * rc, rc)
            e = ext_ref[pl.ds(r0, rc + CONV_HALO), cols]
            acc = jnp.zeros((rc, LANES), F32) + b_ref[:, cols]
            for r in range(SUBLANES):
                taps = [k for k in range(kc) if (base + k) % SUBLANES == r]
                if not taps:
                    continue
                n = rc if r == 0 else rc + SUBLANES
                part = None
                for k in taps:
                    lo = (base + k) // SUBLANES * SUBLANES
                    term = e[lo:lo + n, :] * _tile_rows(w8_ref[k, :, cols], n)
                    part = term if part is None else part + term
                acc = acc + part[r:r + rc, :]
            conv_ref[pl.ds(r0, rc), cols] = acc
            return carry

        lax.fori_loop(0, tm // rc, row_body, 0)

    c = conv_ref[...]
    mu = jnp.mean(c, axis=-1, keepdims=True)
    xc = c - mu
    var = jnp.mean(xc * xc, axis=-1, keepdims=True)
    y = xc * lax.rsqrt(var + LN_EPS) * g_ref[...] + beta_ref[...]
    o_ref[...] = (y * jax.nn.sigmoid(y)).astype(o_ref.dtype)


def _conf_mid(v, w_dw, b_dw, ln_g, ln_b, layer, seq_len, tm):
    t, d = v.shape
    kc = w_dw.shape[1]
    assert kc - 1 <= CONV_HALO and seq_len % tm == 0 and tm % CONV_HALO == 0 and d % LANES == 0
    rc = min(64, tm)
    halo_per_tile = tm // CONV_HALO
    row = lambda m: (layer, 0, 0)
    kern = functools.partial(_conf_mid_kernel, tm=tm, tiles_per_seq=seq_len // tm, rc=rc)
    return pl.pallas_call(
        kern,
        out_shape=jax.ShapeDtypeStruct((t, d), BF16),
        grid=(t // tm,),
        in_specs=[pl.BlockSpec((tm, d), lambda m: (m, 0)),
                  pl.BlockSpec((CONV_HALO, d), lambda m: (jnp.maximum(m * halo_per_tile - 1, 0), 0)),
                  pl.BlockSpec((None, kc, d), row),
                  pl.BlockSpec((None, 1, d), row),
                  pl.BlockSpec((None, 1, d), row),
                  pl.BlockSpec((None, 1, d), row)],
        out_specs=pl.BlockSpec((tm, d), lambda m: (m, 0)),
        scratch_shapes=[pltpu.VMEM((CONV_HALO + tm, d), F32), pltpu.VMEM((tm, d), F32),
                        pltpu.VMEM((kc, SUBLANES, d), F32)],
        compiler_params=_params(("parallel",)),
        name="conformer_conv_ln_swish",
    )(v, v, w_dw, b_dw, ln_g, ln_b)


def _split2(x):
    hi = x.astype(BF16)
    lo = (x - hi.astype(F32)).astype(BF16)
    return hi, lo


def _split3(x):
    hi, lo = _split2(x)
    lo2 = (x - hi.astype(F32) - lo.astype(F32)).astype(BF16)
    return hi, lo, lo2


def _ssd_kernel(x_ref, b_ref, c_ref, z_ref, dt_ref, dtb_ref, alog_ref, dskip_ref, ng_ref, o_ref,
                state_ref, parts_ref, dte_ref, acse_ref, *, tl, gw, gs):
    q = SSD_CHUNK
    lt = pl.program_id(1)
    g0 = pl.program_id(2) * gs
    hpg = gw // SSD_HEAD_DIM

    @pl.when(lt == 0)
    def _():
        for gi in range(gs):
            state_ref[g0 + gi] = jnp.zeros(state_ref.shape[1:], F32)

    @pl.when(g0 == 0)
    def _():
        pre = dt_ref[...] + dtb_ref[...]
        dt = jnp.maximum(pre, 0.0) + jnp.log1p(jnp.exp(-jnp.abs(pre)))
        a = dt * (-jnp.exp(alog_ref[...]) * LOG2_E)
        r = lax.broadcasted_iota(jnp.int32, (tl, tl), 0)
        c = lax.broadcasted_iota(jnp.int32, (tl, tl), 1)
        tri = jnp.where((r >= c) & ((r >> Q_SHIFT) == (c >> Q_SHIFT)), 1.0, 0.0).astype(BF16)
        acs = jnp.zeros((tl, LANES), F32)
        for part in _split3(a):
            acs = acs + jnp.dot(tri, part, preferred_element_type=F32)
        for i, part in enumerate(_split2(dt) + _split2(acs)):
            parts_ref[i * tl:(i + 1) * tl, :] = part

    hrow = lax.broadcasted_iota(jnp.int32, (LANES, gs * gw), 0)
    hcol = lax.broadcasted_iota(jnp.int32, (LANES, gs * gw), 1)
    expand = jnp.where(hrow == g0 * hpg + (hcol >> HEAD_SHIFT), 1.0, 0.0).astype(BF16)
    ex = jnp.dot(parts_ref[...], expand, preferred_element_type=F32)
    dte_ref[...] = ex[0:tl] + ex[tl:2 * tl]
    acse_ref[...] = ex[2 * tl:3 * tl] + ex[3 * tl:4 * tl]

    row = lax.broadcasted_iota(jnp.int32, (q, gw), 0)
    pos = lax.broadcasted_iota(jnp.int32, (q, gw), 1) & (SSD_HEAD_DIM - 1)
    causal = row >= pos
    diagonal = row == pos
    br = lax.broadcasted_iota(jnp.int32, (MXU_DIM, MXU_DIM), 0) >> HEAD_SHIFT
    bc = lax.broadcasted_iota(jnp.int32, (MXU_DIM, MXU_DIM), 1) >> HEAD_SHIFT
    head_mask = jnp.where(br == bc, 1.0, 0.0).astype(BF16)
    heads_per_dot = MXU_DIM // SSD_HEAD_DIM

    for ci in range(tl // q):
        rows = pl.ds(ci * q, q)
        for gi in range(gs):
            lanes = slice(gi * gw, (gi + 1) * gw)
            nlanes = slice(gi * SSD_D_STATE, (gi + 1) * SSD_D_STATE)
            x = x_ref[rows, lanes]
            bm = b_ref[rows, nlanes].astype(BF16)
            cm = c_ref[rows, nlanes].astype(BF16)
            acs = acse_ref[rows, lanes]
            a_last = acse_ref[pl.ds(ci * q + q - 1, 1), lanes]
            xd = x * dte_ref[rows, lanes]
            xd_bf = xd.astype(BF16)

            prev = state_ref[g0 + gi]
            y = jnp.dot(cm, prev.astype(BF16), preferred_element_type=F32) * jnp.exp2(acs)

            wgt = (jnp.exp2(a_last - acs) * xd).astype(BF16)
            s_new = lax.dot_general(bm, wgt, (((0,), (0,)), ((), ())), preferred_element_type=F32)
            state_ref[g0 + gi] = prev * jnp.exp2(a_last) + s_new

            cb = lax.dot_general(cm, jnp.concatenate([bm] * hpg, axis=0), (((1,), (1,)), ((), ())),
                                 preferred_element_type=F32)
            acs_s = jnp.sum(jnp.where(diagonal, acs, 0.0), axis=0, keepdims=True)
            decay = jnp.exp2(jnp.where(causal, acs - acs_s, NEG_BIG))
            mix = (cb * decay).astype(BF16)
            parts = []
            for j in range(gw // MXU_DIM):
                cols = slice(j * MXU_DIM, (j + 1) * MXU_DIM)
                blockdiag = jnp.concatenate([xd_bf[:, cols]] * heads_per_dot, axis=0) * head_mask
                parts.append(jnp.dot(mix[:, cols], blockdiag, preferred_element_type=F32))
            y = y + jnp.concatenate(parts, axis=1) + dskip_ref[:, lanes] * x

            z = z_ref[rows, lanes]
            yg = y * (z * jax.nn.sigmoid(z))
            ms = jnp.mean(yg * yg, axis=-1, keepdims=True)
            o_ref[rows, lanes] = (yg * lax.rsqrt(ms + RMS_EPS) * ng_ref[:, lanes]).astype(o_ref.dtype)


def _ssd(xbc, z, dt, dt_bias, a_log, d_skip, norm_g, layer, batch, seq_len, tl, gs):
    t, d_inner = z.shape
    gw = d_inner // SSD_N_GROUPS
    bw = gs * SSD_D_STATE
    assert gw % MXU_DIM == 0 and SSD_D_STATE == LANES and seq_len % tl == 0 and tl % SSD_CHUNK == 0
    assert SSD_N_GROUPS % gs == 0 and d_inner % bw == 0
    nl = seq_len // tl
    b_blk = d_inner // bw
    c_blk = b_blk + SSD_N_GROUPS // gs
    rows = lambda b, l, g: (b * nl + l, g)
    head_row = lambda b, l, g: (layer, 0, 0)
    group_row = lambda b, l, g: (layer, 0, g)
    kern = functools.partial(_ssd_kernel, tl=tl, gw=gw, gs=gs)
    return pl.pallas_call(
        kern,
        out_shape=jax.ShapeDtypeStruct((t, d_inner), BF16),
        grid=(batch, nl, SSD_N_GROUPS // gs),
        in_specs=[pl.BlockSpec((tl, gs * gw), rows),
                  pl.BlockSpec((tl, bw), lambda b, l, g: (b * nl + l, b_blk + g)),
                  pl.BlockSpec((tl, bw), lambda b, l, g: (b * nl + l, c_blk + g)),
                  pl.BlockSpec((tl, gs * gw), rows),
                  pl.BlockSpec((tl, LANES), lambda b, l, g: (b * nl + l, 0)),
                  pl.BlockSpec((None, 1, LANES), head_row),
                  pl.BlockSpec((None, 1, LANES), head_row),
                  pl.BlockSpec((None, 1, gs * gw), group_row),
                  pl.BlockSpec((None, 1, gs * gw), group_row)],
        out_specs=pl.BlockSpec((tl, gs * gw), rows),
        scratch_shapes=[pltpu.VMEM((SSD_N_GROUPS, SSD_D_STATE, gw), F32),
                        pltpu.VMEM((4 * tl, LANES), BF16),
                        pltpu.VMEM((tl, gs * gw), F32),
                        pltpu.VMEM((tl, gs * gw), F32)],
        compiler_params=_params(("parallel", "arbitrary", "arbitrary")),
        name="ssd_scan_gate_norm",
    )(xbc, xbc, xbc, z, dt, dt_bias, a_log, d_skip, norm_g)


def _row3(p):
    return p.reshape(p.shape[0], 1, p.shape[1])


def _pad_lanes(p):
    return jnp.pad(p, ((0, 0), (0, LANES - p.shape[1])))


def kernel(x, norm_mix_g, norm_ffn_g, norm_final_g, cv_w_in, cv_b_in, cv_w_dw, cv_b_dw, cv_ln_g, cv_ln_b, cv_w_out, cv_b_out, ssm_w_in, ssm_w_conv, ssm_b_conv, ssm_dt_bias, ssm_a_log, ssm_d, ssm_norm_g, ssm_w_out, ffn_w_up, ffn_w_dw, ffn_b_dw, ffn_w_down):
    batch, seq_len, d = x.shape
    t = batch * seq_len
    depth = norm_mix_g.shape[0]
    d_inner = ssm_w_out.shape[1]
    n_heads = ssm_dt_bias.shape[1]
    gn = SSD_N_GROUPS * SSD_D_STATE
    f = ffn_w_down.shape[1]
    assert n_heads * SSD_HEAD_DIM == d_inner and n_heads <= LANES
    assert ssm_w_in.shape[2] == 2 * d_inner + 2 * gn + n_heads

    tm = min(1024, seq_len)
    tm2 = min(2048, seq_len)
    tm_k = min(512, seq_len)
    tm_s = min(256, seq_len)
    tn = min(512, d)
    tn2 = min(1024, d)
    tnorm = min(512, seq_len)
    tconv = min(256, seq_len)
    tl = min(512, seq_len)
    rc = min(256, seq_len)
    rc_t = min(512, seq_len)

    mix_gain = norm_mix_g.reshape(depth, d, 1)
    ffn_gain = norm_ffn_g.reshape(depth, d, 1)
    norm_mix_g = _row3(norm_mix_g)
    norm_final_g = norm_final_g.reshape(1, 1, d)
    cv_b_in, cv_b_dw, cv_ln_g, cv_ln_b, cv_b_out = map(_row3, (cv_b_in, cv_b_dw, cv_ln_g, cv_ln_b, cv_b_out))
    ssm_b_conv, ssm_norm_g, ffn_b_dw = map(_row3, (ssm_b_conv, ssm_norm_g, ffn_b_dw))
    dt_bias = _row3(_pad_lanes(ssm_dt_bias))
    a_log = _row3(_pad_lanes(ssm_a_log))
    d_skip = _row3(jnp.repeat(ssm_d, SSD_HEAD_DIM, axis=1))
    ssm_w_in_t = jnp.swapaxes(ssm_w_in, 1, 2)

    xf = x.reshape(t, d)
    for i in range(depth):
        j = i // 2
        if i % 2 == 0:
            v = _mm(xf, cv_w_in, j, mode="glu", col_offs=(0, d), n_cols=d, tn=tn, tm=tm, rc=rc, seq_len=seq_len,
                    gain=mix_gain, gain_layer=i, bias=cv_b_in, name="conformer_in_glu")
            s = _conf_mid(v, cv_w_dw, cv_b_dw, cv_ln_g, cv_ln_b, j, seq_len, tconv)
            xf = _mm(s, cv_w_out, j, mode="res_bias", col_offs=(0,), n_cols=d, tn=tn2, tm=tm, rc=rc,
                     seq_len=seq_len, res=xf, bias=cv_b_out, name="conformer_out")
        else:
            h = _rmsnorm(xf, norm_mix_g, i, BF16, tnorm)
            z = _mm(h, ssm_w_in_t, j, mode="plain", col_offs=(0,), n_cols=d_inner, tn=tn2, tm=tm, rc=rc_t,
                    seq_len=seq_len, transposed=True, name="ssm_in_z")
            xbc = _mm(h, ssm_w_in_t, j, mode="conv_silu", col_offs=(d_inner,), n_cols=d_inner + 2 * gn, tn=tn2,
                      tm=tm, rc=rc, seq_len=seq_len, transposed=True, conv_w=ssm_w_conv, conv_b=ssm_b_conv,
                      name="ssm_in_xbc")
            dt = _mm(h, ssm_w_in_t, j, mode="mask_cols", col_offs=(2 * d_inner + 2 * gn,), n_cols=n_heads,
                     tn=LANES, tm=tm, rc=rc_t, seq_len=seq_len, transposed=True, name="ssm_in_dt")
            yn = _ssd(xbc, z, dt, dt_bias, a_log, d_skip, ssm_norm_g, j, batch, seq_len, tl, SSD_GROUPS_PER_STEP)
            xf = _mm(yn, ssm_w_out, j, mode="res", col_offs=(0,), n_cols=d, tn=tn2, tm=tm_s, rc=tm_s,
                     seq_len=seq_len, res=xf, name="ssm_out")
        act = _mm(xf, ffn_w_up, i, mode="ffn", col_offs=(0, f), n_cols=f, tn=tn, tm=tm, rc=rc, seq_len=seq_len,
                  out_dtype=BF16, gain=ffn_gain, gain_layer=i, conv_w=ffn_w_dw, conv_b=ffn_b_dw,
                  name="ffn_up_conv_gate")
        xf = _mm(act, ffn_w_down, i, mode="res", col_offs=(0,), n_cols=d, tn=tn, tm=tm_k, rc=rc, seq_len=seq_len,
                 res=xf, name="ffn_down")
    out = _rmsnorm(xf, norm_final_g, 0, F32, tnorm)
    return out.reshape(batch, seq_len, d)
```

```python
import functools

import jax
import jax.numpy as jnp
from jax import lax
from jax.experimental import pallas as pl
from jax.experimental.pallas import tpu as pltpu

RMS_EPS = 1e-6
LN_EPS = 1e-5

SSD_CHUNK = 64
SSD_HEAD_DIM = 64
SSD_N_GROUPS = 8
SSD_D_STATE = 128
SSD_GROUPS_PER_STEP = 4
Q_SHIFT = SSD_CHUNK.bit_length() - 1
HEAD_SHIFT = SSD_HEAD_DIM.bit_length() - 1

LANES = 128
SUBLANES = 8
MXU_DIM = 256
VMEM_LIMIT_BYTES = 56 * 1024 * 1024

CONV_HALO = 32
NEG_BIG = -1e30
LOG2_E = 1.4426950408889634

F32 = jnp.float32
BF16 = jnp.bfloat16


def _params(semantics):
    return pltpu.CompilerParams(dimension_semantics=semantics, vmem_limit_bytes=VMEM_LIMIT_BYTES)


def _rmsnorm_kernel(x_ref, g_ref, o_ref):
    x = x_ref[...]
    ms = jnp.mean(x * x, axis=-1, keepdims=True)
    o_ref[...] = (x * lax.rsqrt(ms + RMS_EPS) * g_ref[...]).astype(o_ref.dtype)


def _rmsnorm(x, gains, layer, out_dtype, tm):
    t, d = x.shape
    return pl.pallas_call(
        _rmsnorm_kernel,
        out_shape=jax.ShapeDtypeStruct((t, d), out_dtype),
        grid=(t // tm,),
        in_specs=[pl.BlockSpec((tm, d), lambda m: (m, 0)),
                  pl.BlockSpec((None, 1, d), lambda m: (layer, 0, 0))],
        out_specs=pl.BlockSpec((tm, d), lambda m: (m, 0)),
        compiler_params=_params(("parallel",)),
        name="rmsnorm",
    )(x, gains)


def _tile_rows(x8, rows):
    return jnp.tile(x8, (rows // SUBLANES, 1))


def _causal_conv_rows(u, tail, p8_ref, kc):
    rc = u.shape[0]
    ext = jnp.concatenate([tail, u], axis=0)
    out = u * _tile_rows(p8_ref[kc - 1], rc) + _tile_rows(p8_ref[kc], rc)
    for k in range(kc - 1):
        start = SUBLANES - (kc - 1 - k)
        out = out + ext[start:start + rc, :] * _tile_rows(p8_ref[k], rc)
    return out


def _mm_kernel(*refs, mode, nw, tiles_per_seq, n_valid, rc, transposed, norm):
    a_ref = refs[0]
    refs = refs[1:]
    if norm:
        gain_ref, rstd_ref = refs[0], refs[-1]
        refs = refs[1:-1]
    w_refs = refs[:nw]
    rest = refs[nw:]
    m = pl.program_id(1)
    tm = a_ref.shape[0]
    conv_refs, p8_refs, carry_ref = (), (), None

    if mode == "glu":
        ba_ref, bg_ref, o_ref, *wb_refs = rest
    elif mode == "res":
        res_ref, o_ref, *wb_refs = rest
    elif mode == "res_bias":
        res_ref, b_ref, o_ref, *wb_refs = rest
    elif mode == "plain" or mode == "mask_cols":
        o_ref, *wb_refs = rest
    elif mode == "conv_silu":
        cw_ref, cb_ref, o_ref, wb0, p8, carry_ref = rest
        wb_refs, conv_refs, p8_refs = [wb0], [(cw_ref, cb_ref)], [p8]
    elif mode == "ffn":
        cwg_ref, cwv_ref, cbg_ref, cbv_ref, o_ref, wb0, wb1, p8g, p8v, carry_ref = rest
        wb_refs, conv_refs, p8_refs = [wb0, wb1], [(cwg_ref, cbg_ref), (cwv_ref, cbv_ref)], [p8g, p8v]
    else:
        raise ValueError(mode)
    kc = conv_refs[0][0].shape[0] if conv_refs else 0

    @pl.when(m == 0)
    def _():
        for w_ref, wb_ref in zip(w_refs, wb_refs):
            w = w_ref[...] * gain_ref[...] if norm else w_ref[...]
            wb_ref[...] = (w.T if transposed else w).astype(BF16)
        for (cw_ref, cb_ref), p8_ref in zip(conv_refs, p8_refs):
            for k in range(kc):
                p8_ref[k] = jnp.broadcast_to(cw_ref[k:k + 1, :], p8_ref.shape[1:])
            p8_ref[kc] = jnp.broadcast_to(cb_ref[...], p8_ref.shape[1:])

    if conv_refs:
        @pl.when((m % tiles_per_seq) == 0)
        def _():
            carry_ref[...] = jnp.zeros(carry_ref.shape, F32)

    if norm:
        @pl.when(pl.program_id(0) == 0)
        def _():
            x = a_ref[...]
            ms = jnp.mean(x * x, axis=-1, keepdims=True)
            rstd_ref[m] = jnp.broadcast_to(lax.rsqrt(ms + RMS_EPS), rstd_ref.shape[1:])

    def chunk_matmuls(c):
        rows = slice(c * rc, (c + 1) * rc)
        a = a_ref[rows, :].astype(BF16)
        accs = [jnp.dot(a, wb_ref[...], preferred_element_type=F32) for wb_ref in wb_refs]
        if norm:
            scale = jnp.tile(rstd_ref[m, rows, :], (1, accs[0].shape[1] // LANES))
            accs = [acc * scale for acc in accs]
        return accs

    tails = [carry_ref[i] for i in range(len(conv_refs))]
    n_chunks = tm // rc
    accs_next = chunk_matmuls(0)
    for c in range(n_chunks):
        rows = slice(c * rc, (c + 1) * rc)
        accs = accs_next
        if c + 1 < n_chunks:
            accs_next = chunk_matmuls(c + 1)
        if mode == "glu":
            o_ref[rows, :] = (accs[0] + ba_ref[...]) * jax.nn.sigmoid(accs[1] + bg_ref[...])
        elif mode == "res":
            o_ref[rows, :] = res_ref[rows, :] + accs[0]
        elif mode == "res_bias":
            o_ref[rows, :] = res_ref[rows, :] + (accs[0] + b_ref[...])
        elif mode == "plain":
            o_ref[rows, :] = accs[0].astype(o_ref.dtype)
        elif mode == "mask_cols":
            lane = lax.broadcasted_iota(jnp.int32, accs[0].shape, 1)
            o_ref[rows, :] = jnp.where(lane < n_valid, accs[0], 0.0)
        else:
            convs = [_causal_conv_rows(u, tail, p8_ref, kc) for u, tail, p8_ref in zip(accs, tails, p8_refs)]
            tails = [u[rc - SUBLANES:rc, :] for u in accs]
            if mode == "conv_silu":
                o_ref[rows, :] = convs[0] * jax.nn.sigmoid(convs[0])
            else:
                o_ref[rows, :] = (convs[0] * jax.nn.sigmoid(convs[0]) * convs[1]).astype(o_ref.dtype)
    for i, tail in enumerate(tails):
        carry_ref[i] = tail


def _mm(a, w, layer, *, mode, col_offs, n_cols, tn, tm, rc, seq_len, out_dtype=F32, transposed=False,
        gain=None, gain_layer=None, bias=None, conv_w=None, conv_b=None, res=None, name):
    t, k = a.shape
    nw = len(col_offs)
    norm = gain is not None
    assert not (norm and transposed)
    assert t % tm == 0 and seq_len % tm == 0 and tm % rc == 0 and all(off % tn == 0 for off in col_offs)
    n_tiles = pl.cdiv(n_cols, tn)
    out_cols = n_tiles * tn
    offs = [off // tn for off in col_offs]

    def col_spec(rows, ob):
        return pl.BlockSpec((None, rows, tn), lambda n, m: (layer, 0, n + ob))

    def weight_spec(ob):
        if transposed:
            return pl.BlockSpec((None, tn, k), lambda n, m: (layer, n + ob, 0))
        return col_spec(k, ob)

    in_specs = [pl.BlockSpec((tm, k), lambda n, m: (m, 0))]
    args = [a]
    if norm:
        in_specs.append(pl.BlockSpec((None, k, 1), lambda n, m: (gain_layer, 0, 0)))
        args.append(gain)
    for ob in offs:
        in_specs.append(weight_spec(ob))
        args.append(w)
    tile_spec = pl.BlockSpec((tm, tn), lambda n, m: (m, n))
    scratch = [pltpu.VMEM((k, tn), BF16) for _ in offs]

    if mode == "glu":
        in_specs += [col_spec(1, ob) for ob in offs]
        args += [bias, bias]
    elif mode == "res":
        in_specs.append(tile_spec)
        args.append(res)
    elif mode == "res_bias":
        in_specs += [tile_spec, col_spec(1, offs[0])]
        args += [res, bias]
    elif mode == "conv_silu":
        kc = conv_w.shape[1]
        in_specs += [col_spec(kc, 0), col_spec(1, 0)]
        args += [conv_w, conv_b]
    elif mode == "ffn":
        kc = conv_w.shape[1]
        in_specs += [col_spec(kc, ob) for ob in offs] + [col_spec(1, ob) for ob in offs]
        args += [conv_w, conv_w, conv_b, conv_b]
    if mode in ("conv_silu", "ffn"):
        assert kc - 1 <= SUBLANES
        scratch += [pltpu.VMEM((kc + 1, SUBLANES, tn), F32) for _ in offs]
        scratch += [pltpu.VMEM((nw, SUBLANES, tn), F32)]
    if norm:
        scratch += [pltpu.VMEM((t // tm, tm, LANES), F32)]

    kern = functools.partial(_mm_kernel, mode=mode, nw=nw, tiles_per_seq=seq_len // tm, n_valid=n_cols,
                             rc=rc, transposed=transposed, norm=norm)
    return pl.pallas_call(
        kern,
        out_shape=jax.ShapeDtypeStruct((t, out_cols), out_dtype),
        grid=(n_tiles, t // tm),
        in_specs=in_specs,
        out_specs=tile_spec,
        scratch_shapes=scratch,
        compiler_params=_params(("arbitrary" if norm else "parallel", "arbitrary")),
        name=name,
    )(*args)


def _conf_mid_kernel(cur_ref, halo_ref, w_ref, b_ref, g_ref, beta_ref, o_ref, ext_ref, conv_ref, w8_ref,
                     *, tm, tiles_per_seq, rc):
    kc = w_ref.shape[0]
    d = cur_ref.shape[1]
    seq_start = (pl.program_id(0) % tiles_per_seq) == 0

    @pl.when(seq_start)
    def _():
        ext_ref[0:CONV_HALO, :] = jnp.zeros((CONV_HALO, d), F32)

    @pl.when(jnp.logical_not(seq_start))
    def _():
        ext_ref[0:CONV_HALO, :] = halo_ref[...]

    ext_ref[CONV_HALO:CONV_HALO + tm, :] = cur_ref[...]
    for k in range(kc):
        w8_ref[k] = jnp.broadcast_to(w_ref[k:k + 1, :], (SUBLANES, d))

    base = CONV_HALO - (kc - 1)
    for cj in range(d // LANES):
        cols = slice(cj * LANES, (cj + 1) * LANES)

        def row_body(ri, carry, cols=cols):
            r0 = pl.multiple_of(ri * rc, rc)
            e = ext_ref[pl.ds(r0, rc + CONV_HALO), cols]
            acc = jnp.zeros((rc, LANES), F32) + b_ref[:, cols]
            for r in range(SUBLANES):
                taps = [k for k in range(kc) if (base + k) % SUBLANES == r]
                if not taps:
                    continue
                n = rc if r == 0 else rc + SUBLANES
                part = None
                for k in taps:
                    lo = (base + k) // SUBLANES * SUBLANES
                    term = e[lo:lo + n, :] * _tile_rows(w8_ref[k, :, cols], n)
                    part = term if part is None else part + term
                acc = acc + part[r:r + rc, :]
            conv_ref[pl.ds(r0, rc), cols] = acc
            return carry

        lax.fori_loop(0, tm // rc, row_body, 0)

    c = conv_ref[...]
    mu = jnp.mean(c, axis=-1, keepdims=True)
    xc = c - mu
    var = jnp.mean(xc * xc, axis=-1, keepdims=True)
    y = xc * lax.rsqrt(var + LN_EPS) * g_ref[...] + beta_ref[...]
    o_ref[...] = (y * jax.nn.sigmoid(y)).astype(o_ref.dtype)


def _conf_mid(v, w_dw, b_dw, ln_g, ln_b, layer, seq_len, tm):
    t, d = v.shape
    kc = w_dw.shape[1]
    assert kc - 1 <= CONV_HALO and seq_len % tm == 0 and tm % CONV_HALO == 0 and d % LANES == 0
    rc = min(128, tm)
    halo_per_tile = tm // CONV_HALO
    row = lambda m: (layer, 0, 0)
    kern = functools.partial(_conf_mid_kernel, tm=tm, tiles_per_seq=seq_len // tm, rc=rc)
    return pl.pallas_call(
        kern,
        out_shape=jax.ShapeDtypeStruct((t, d), BF16),
        grid=(t // tm,),
        in_specs=[pl.BlockSpec((tm, d), lambda m: (m, 0)),
                  pl.BlockSpec((CONV_HALO, d), lambda m: (jnp.maximum(m * halo_per_tile - 1, 0), 0)),
                  pl.BlockSpec((None, kc, d), row),
                  pl.BlockSpec((None, 1, d), row),
                  pl.BlockSpec((None, 1, d), row),
                  pl.BlockSpec((None, 1, d), row)],
        out_specs=pl.BlockSpec((tm, d), lambda m: (m, 0)),
        scratch_shapes=[pltpu.VMEM((CONV_HALO + tm, d), F32), pltpu.VMEM((tm, d), F32),
                        pltpu.VMEM((kc, SUBLANES, d), F32)],
        compiler_params=_params(("parallel",)),
        name="conformer_conv_ln_swish",
    )(v, v, w_dw, b_dw, ln_g, ln_b)


def _split2(x):
    hi = x.astype(BF16)
    lo = (x - hi.astype(F32)).astype(BF16)
    return hi, lo


def _split3(x):
    hi, lo = _split2(x)
    lo2 = (x - hi.astype(F32) - lo.astype(F32)).astype(BF16)
    return hi, lo, lo2


def _ssd_kernel(x_ref, b_ref, c_ref, z_ref, dt_ref, dtb_ref, alog_ref, dskip_ref, ng_ref, o_ref,
                state_ref, parts_ref, dte_ref, acse_ref, *, tl, gw, gs):
    q = SSD_CHUNK
    lt = pl.program_id(1)
    g0 = pl.program_id(2) * gs
    hpg = gw // SSD_HEAD_DIM

    @pl.when(lt == 0)
    def _():
        for gi in range(gs):
            state_ref[g0 + gi] = jnp.zeros(state_ref.shape[1:], F32)

    @pl.when(g0 == 0)
    def _():
        pre = dt_ref[...] + dtb_ref[...]
        dt = jnp.maximum(pre, 0.0) + jnp.log1p(jnp.exp(-jnp.abs(pre)))
        a = dt * (-jnp.exp(alog_ref[...]) * LOG2_E)
        r = lax.broadcasted_iota(jnp.int32, (tl, tl), 0)
        c = lax.broadcasted_iota(jnp.int32, (tl, tl), 1)
        tri = jnp.where((r >= c) & ((r >> Q_SHIFT) == (c >> Q_SHIFT)), 1.0, 0.0).astype(BF16)
        acs = jnp.zeros((tl, LANES), F32)
        for part in _split3(a):
            acs = acs + jnp.dot(tri, part, preferred_element_type=F32)
        for i, part in enumerate(_split2(dt) + _split2(acs)):
            parts_ref[i * tl:(i + 1) * tl, :] = part

    hrow = lax.broadcasted_iota(jnp.int32, (LANES, gs * gw), 0)
    hcol = lax.broadcasted_iota(jnp.int32, (LANES, gs * gw), 1)
    expand = jnp.where(hrow == g0 * hpg + (hcol >> HEAD_SHIFT), 1.0, 0.0).astype(BF16)
    ex = jnp.dot(parts_ref[...], expand, preferred_element_type=F32)
    dte_ref[...] = ex[0:tl] + ex[tl:2 * tl]
    acse_ref[...] = ex[2 * tl:3 * tl] + ex[3 * tl:4 * tl]

    row = lax.broadcasted_iota(jnp.int32, (q, gw), 0)
    pos = lax.broadcasted_iota(jnp.int32, (q, gw), 1) & (SSD_HEAD_DIM - 1)
    causal = row >= pos
    diagonal = row == pos
    br = lax.broadcasted_iota(jnp.int32, (MXU_DIM, MXU_DIM), 0) >> HEAD_SHIFT
    bc = lax.broadcasted_iota(jnp.int32, (MXU_DIM, MXU_DIM), 1) >> HEAD_SHIFT
    head_mask = jnp.where(br == bc, 1.0, 0.0).astype(BF16)
    heads_per_dot = MXU_DIM // SSD_HEAD_DIM

    for ci in range(tl // q):
        rows = pl.ds(ci * q, q)
        for gi in range(gs):
            lanes = slice(gi * gw, (gi + 1) * gw)
            nlanes = slice(gi * SSD_D_STATE, (gi + 1) * SSD_D_STATE)
            x = x_ref[rows, lanes]
            bm = b_ref[rows, nlanes].astype(BF16)
            cm = c_ref[rows, nlanes].astype(BF16)
            acs = acse_ref[rows, lanes]
            a_last = acse_ref[pl.ds(ci * q + q - 1, 1), lanes]
            xd = x * dte_ref[rows, lanes]
            xd_bf = xd.astype(BF16)

            prev = state_ref[g0 + gi]
            y = jnp.dot(cm, prev.astype(BF16), preferred_element_type=F32) * jnp.exp2(acs)

            wgt = (jnp.exp2(a_last - acs) * xd).astype(BF16)
            s_new = lax.dot_general(bm, wgt, (((0,), (0,)), ((), ())), preferred_element_type=F32)
            state_ref[g0 + gi] = prev * jnp.exp2(a_last) + s_new

            cb = lax.dot_general(cm, jnp.concatenate([bm] * hpg, axis=0), (((1,), (1,)), ((), ())),
                                 preferred_element_type=F32)
            acs_s = jnp.sum(jnp.where(diagonal, acs, 0.0), axis=0, keepdims=True)
            decay = jnp.exp2(jnp.where(causal, acs - acs_s, NEG_BIG))
            mix = (cb * decay).astype(BF16)
            parts = []
            for j in range(gw // MXU_DIM):
                cols = slice(j * MXU_DIM, (j + 1) * MXU_DIM)
                blockdiag = jnp.concatenate([xd_bf[:, cols]] * heads_per_dot, axis=0) * head_mask
                parts.append(jnp.dot(mix[:, cols], blockdiag, preferred_element_type=F32))
            y = y + jnp.concatenate(parts, axis=1) + dskip_ref[:, lanes] * x

            z = z_ref[rows, lanes]
            yg = y * (z * jax.nn.sigmoid(z))
            ms = jnp.mean(yg * yg, axis=-1, keepdims=True)
            o_ref[rows, lanes] = (yg * lax.rsqrt(ms + RMS_EPS) * ng_ref[:, lanes]).astype(o_ref.dtype)


def _ssd(xbc, z, dt, dt_bias, a_log, d_skip, norm_g, layer, batch, seq_len, tl, gs):
    t, d_inner = z.shape
    gw = d_inner // SSD_N_GROUPS
    bw = gs * SSD_D_STATE
    assert gw % MXU_DIM == 0 and SSD_D_STATE == LANES and seq_len % tl == 0 and tl % SSD_CHUNK == 0
    assert SSD_N_GROUPS % gs == 0 and d_inner % bw == 0
    nl = seq_len // tl
    b_blk = d_inner // bw
    c_blk = b_blk + SSD_N_GROUPS // gs
    rows = lambda b, l, g: (b * nl + l, g)
    head_row = lambda b, l, g: (layer, 0, 0)
    group_row = lambda b, l, g: (layer, 0, g)
    kern = functools.partial(_ssd_kernel, tl=tl, gw=gw, gs=gs)
    return pl.pallas_call(
        kern,
        out_shape=jax.ShapeDtypeStruct((t, d_inner), BF16),
        grid=(batch, nl, SSD_N_GROUPS // gs),
        in_specs=[pl.BlockSpec((tl, gs * gw), rows),
                  pl.BlockSpec((tl, bw), lambda b, l, g: (b * nl + l, b_blk + g)),
                  pl.BlockSpec((tl, bw), lambda b, l, g: (b * nl + l, c_blk + g)),
                  pl.BlockSpec((tl, gs * gw), rows),
                  pl.BlockSpec((tl, LANES), lambda b, l, g: (b * nl + l, 0)),
                  pl.BlockSpec((None, 1, LANES), head_row),
                  pl.BlockSpec((None, 1, LANES), head_row),
                  pl.BlockSpec((None, 1, gs * gw), group_row),
                  pl.BlockSpec((None, 1, gs * gw), group_row)],
        out_specs=pl.BlockSpec((tl, gs * gw), rows),
        scratch_shapes=[pltpu.VMEM((SSD_N_GROUPS, SSD_D_STATE, gw), F32),
                        pltpu.VMEM((4 * tl, LANES), BF16),
                        pltpu.VMEM((tl, gs * gw), F32),
                        pltpu.VMEM((tl, gs * gw), F32)],
        compiler_params=_params(("parallel", "arbitrary", "arbitrary")),
        name="ssd_scan_gate_norm",
    )(xbc, xbc, xbc, z, dt, dt_bias, a_log, d_skip, norm_g)


def _row3(p):
    return p.reshape(p.shape[0], 1, p.shape[1])


def _pad_lanes(p):
    return jnp.pad(p, ((0, 0), (0, LANES - p.shape[1])))


def kernel(x, norm_mix_g, norm_ffn_g, norm_final_g, cv_w_in, cv_b_in, cv_w_dw, cv_b_dw, cv_ln_g, cv_ln_b, cv_w_out, cv_b_out, ssm_w_in, ssm_w_conv, ssm_b_conv, ssm_dt_bias, ssm_a_log, ssm_d, ssm_norm_g, ssm_w_out, ffn_w_up, ffn_w_dw, ffn_b_dw, ffn_w_down):
    batch, seq_len, d = x.shape
    t = batch * seq_len
    depth = norm_mix_g.shape[0]
    d_inner = ssm_w_out.shape[1]
    n_heads = ssm_dt_bias.shape[1]
    gn = SSD_N_GROUPS * SSD_D_STATE
    f = ffn_w_down.shape[1]
    assert n_heads * SSD_HEAD_DIM == d_inner and n_heads <= LANES
    assert ssm_w_in.shape[2] == 2 * d_inner + 2 * gn + n_heads

    tm = min(1024, seq_len)
    tm2 = min(2048, seq_len)
    tm_k = min(512, seq_len)
    tm_s = min(256, seq_len)
    tn = min(512, d)
    tn2 = min(1024, d)
    tnorm = min(512, seq_len)
    tconv = min(512, seq_len)
    tl = min(512, seq_len)
    rc = min(256, seq_len)
    rc_t = min(512, seq_len)

    mix_gain = norm_mix_g.reshape(depth, d, 1)
    ffn_gain = norm_ffn_g.reshape(depth, d, 1)
    norm_mix_g = _row3(norm_mix_g)
    norm_final_g = norm_final_g.reshape(1, 1, d)
    cv_b_in, cv_b_dw, cv_ln_g, cv_ln_b, cv_b_out = map(_row3, (cv_b_in, cv_b_dw, cv_ln_g, cv_ln_b, cv_b_out))
    ssm_b_conv, ssm_norm_g, ffn_b_dw = map(_row3, (ssm_b_conv, ssm_norm_g, ffn_b_dw))
    dt_bias = _row3(_pad_lanes(ssm_dt_bias))
    a_log = _row3(_pad_lanes(ssm_a_log))
    d_skip = _row3(jnp.repeat(ssm_d, SSD_HEAD_DIM, axis=1))
    ssm_w_in_t = jnp.swapaxes(ssm_w_in, 1, 2)

    xf = x.reshape(t, d)
    for i in range(depth):
        j = i // 2
        if i % 2 == 0:
            v = _mm(xf, cv_w_in, j, mode="glu", col_offs=(0, d), n_cols=d, tn=tn, tm=tm, rc=rc, seq_len=seq_len,
                    gain=mix_gain, gain_layer=i, bias=cv_b_in, name="conformer_in_glu")
            s = _conf_mid(v, cv_w_dw, cv_b_dw, cv_ln_g, cv_ln_b, j, seq_len, tconv)
            xf = _mm(s, cv_w_out, j, mode="res_bias", col_offs=(0,), n_cols=d, tn=tn2, tm=tm, rc=rc,
                     seq_len=seq_len, res=xf, bias=cv_b_out, name="conformer_out")
        else:
            h = _rmsnorm(xf, norm_mix_g, i, BF16, tnorm)
            z = _mm(h, ssm_w_in_t, j, mode="plain", col_offs=(0,), n_cols=d_inner, tn=tn2, tm=tm, rc=rc_t,
                    seq_len=seq_len, transposed=True, name="ssm_in_z")
            xbc = _mm(h, ssm_w_in_t, j, mode="conv_silu", col_offs=(d_inner,), n_cols=d_inner + 2 * gn, tn=tn2,
                      tm=tm, rc=rc, seq_len=seq_len, transposed=True, conv_w=ssm_w_conv, conv_b=ssm_b_conv,
                      name="ssm_in_xbc")
            dt = _mm(h, ssm_w_in_t, j, mode="mask_cols", col_offs=(2 * d_inner + 2 * gn,), n_cols=n_heads,
                     tn=LANES, tm=tm, rc=rc_t, seq_len=seq_len, transposed=True, name="ssm_in_dt")
            yn = _ssd(xbc, z, dt, dt_bias, a_log, d_skip, ssm_norm_g, j, batch, seq_len, tl, SSD_GROUPS_PER_STEP)
            xf = _mm(yn, ssm_w_out, j, mode="res", col_offs=(0,), n_cols=d, tn=tn2, tm=tm_s, rc=tm_s,
                     seq_len=seq_len, res=xf, name="ssm_out")
        act = _mm(xf, ffn_w_up, i, mode="ffn", col_offs=(0, f), n_cols=f, tn=tn, tm=tm, rc=rc // 2, seq_len=seq_len,
                  out_dtype=BF16, gain=ffn_gain, gain_layer=i, conv_w=ffn_w_dw, conv_b=ffn_b_dw,
                  name="ffn_up_conv_gate")
        xf = _mm(act, ffn_w_down, i, mode="res", col_offs=(0,), n_cols=d, tn=tn, tm=tm_k, rc=rc, seq_len=seq_len,
                 res=xf, name="ffn_down")
    out = _rmsnorm(xf, norm_final_g, 0, F32, tnorm)
    return out.reshape(batch, seq_len, d)
```

```python
import functools

import jax
import jax.numpy as jnp
from jax import lax
from jax.experimental import pallas as pl
from jax.experimental.pallas import tpu as pltpu

RMS_EPS = 1e-6
LN_EPS = 1e-5

SSD_CHUNK = 64
SSD_HEAD_DIM = 64
SSD_N_GROUPS = 8
SSD_D_STATE = 128
SSD_GROUPS_PER_STEP = 4
Q_SHIFT = SSD_CHUNK.bit_length() - 1
HEAD_SHIFT = SSD_HEAD_DIM.bit_length() - 1

LANES = 128
SUBLANES = 8
MXU_DIM = 256
VMEM_LIMIT_BYTES = 56 * 1024 * 1024

CONV_HALO = 32
NEG_BIG = -1e30
LOG2_E = 1.4426950408889634

F32 = jnp.float32
BF16 = jnp.bfloat16


def _params(semantics):
    return pltpu.CompilerParams(dimension_semantics=semantics, vmem_limit_bytes=VMEM_LIMIT_BYTES)


def _rmsnorm_kernel(x_ref, g_ref, o_ref):
    x = x_ref[...]
    ms = jnp.mean(x * x, axis=-1, keepdims=True)
    o_ref[...] = (x * lax.rsqrt(ms + RMS_EPS) * g_ref[...]).astype(o_ref.dtype)


def _rmsnorm(x, gains, layer, out_dtype, tm):
    t, d = x.shape
    return pl.pallas_call(
        _rmsnorm_kernel,
        out_shape=jax.ShapeDtypeStruct((t, d), out_dtype),
        grid=(t // tm,),
        in_specs=[pl.BlockSpec((tm, d), lambda m: (m, 0)),
                  pl.BlockSpec((None, 1, d), lambda m: (layer, 0, 0))],
        out_specs=pl.BlockSpec((tm, d), lambda m: (m, 0)),
        compiler_params=_params(("parallel",)),
        name="rmsnorm",
    )(x, gains)


def _tile_rows(x8, rows):
    return jnp.tile(x8, (rows // SUBLANES, 1))


def _causal_conv_rows(u, tail, p8_ref, kc):
    rc = u.shape[0]
    ext = jnp.concatenate([tail, u], axis=0)
    out = u * _tile_rows(p8_ref[kc - 1], rc) + _tile_rows(p8_ref[kc], rc)
    for k in range(kc - 1):
        start = SUBLANES - (kc - 1 - k)
        out = out + ext[start:start + rc, :] * _tile_rows(p8_ref[k], rc)
    return out


def _mm_kernel(*refs, mode, nw, tiles_per_seq, n_valid, rc, transposed, norm):
    a_ref = refs[0]
    refs = refs[1:]
    if norm:
        gain_ref, rstd_ref = refs[0], refs[-1]
        refs = refs[1:-1]
    w_refs = refs[:nw]
    rest = refs[nw:]
    m = pl.program_id(1)
    tm = a_ref.shape[0]
    conv_refs, p8_refs, carry_ref = (), (), None

    if mode == "glu":
        ba_ref, bg_ref, o_ref, *wb_refs = rest
    elif mode == "res":
        res_ref, o_ref, *wb_refs = rest
    elif mode == "res_bias":
        res_ref, b_ref, o_ref, *wb_refs = rest
    elif mode == "plain" or mode == "mask_cols":
        o_ref, *wb_refs = rest
    elif mode == "conv_silu":
        cw_ref, cb_ref, o_ref, wb0, p8, carry_ref = rest
        wb_refs, conv_refs, p8_refs = [wb0], [(cw_ref, cb_ref)], [p8]
    elif mode == "ffn":
        cwg_ref, cwv_ref, cbg_ref, cbv_ref, o_ref, wb0, wb1, p8g, p8v, carry_ref = rest
        wb_refs, conv_refs, p8_refs = [wb0, wb1], [(cwg_ref, cbg_ref), (cwv_ref, cbv_ref)], [p8g, p8v]
    else:
        raise ValueError(mode)
    kc = conv_refs[0][0].shape[0] if conv_refs else 0

    @pl.when(m == 0)
    def _():
        for w_ref, wb_ref in zip(w_refs, wb_refs):
            w = w_ref[...] * gain_ref[...] if norm else w_ref[...]
            wb_ref[...] = (w.T if transposed else w).astype(BF16)
        for (cw_ref, cb_ref), p8_ref in zip(conv_refs, p8_refs):
            for k in range(kc):
                p8_ref[k] = jnp.broadcast_to(cw_ref[k:k + 1, :], p8_ref.shape[1:])
            p8_ref[kc] = jnp.broadcast_to(cb_ref[...], p8_ref.shape[1:])

    if conv_refs:
        @pl.when((m % tiles_per_seq) == 0)
        def _():
            carry_ref[...] = jnp.zeros(carry_ref.shape, F32)

    if norm:
        @pl.when(pl.program_id(0) == 0)
        def _():
            x = a_ref[...]
            ms = jnp.mean(x * x, axis=-1, keepdims=True)
            rstd_ref[m] = jnp.broadcast_to(lax.rsqrt(ms + RMS_EPS), rstd_ref.shape[1:])

    def chunk_matmuls(c):
        rows = slice(c * rc, (c + 1) * rc)
        a = a_ref[rows, :].astype(BF16)
        accs = [jnp.dot(a, wb_ref[...], preferred_element_type=F32) for wb_ref in wb_refs]
        if norm:
            scale = jnp.tile(rstd_ref[m, rows, :], (1, accs[0].shape[1] // LANES))
            accs = [acc * scale for acc in accs]
        return accs

    tails = [carry_ref[i] for i in range(len(conv_refs))]
    n_chunks = tm // rc
    accs_next = chunk_matmuls(0)
    for c in range(n_chunks):
        rows = slice(c * rc, (c + 1) * rc)
        accs = accs_next
        if c + 1 < n_chunks:
            accs_next = chunk_matmuls(c + 1)
        if mode == "glu":
            o_ref[rows, :] = (accs[0] + ba_ref[...]) * jax.nn.sigmoid(accs[1] + bg_ref[...])
        elif mode == "res":
            o_ref[rows, :] = res_ref[rows, :] + accs[0]
        elif mode == "res_bias":
            o_ref[rows, :] = res_ref[rows, :] + (accs[0] + b_ref[...])
        elif mode == "plain":
            o_ref[rows, :] = accs[0].astype(o_ref.dtype)
        elif mode == "mask_cols":
            lane = lax.broadcasted_iota(jnp.int32, accs[0].shape, 1)
            o_ref[rows, :] = jnp.where(lane < n_valid, accs[0], 0.0)
        else:
            convs = [_causal_conv_rows(u, tail, p8_ref, kc) for u, tail, p8_ref in zip(accs, tails, p8_refs)]
            tails = [u[rc - SUBLANES:rc, :] for u in accs]
            if mode == "conv_silu":
                o_ref[rows, :] = convs[0] * jax.nn.sigmoid(convs[0])
            else:
                o_ref[rows, :] = (convs[0] * jax.nn.sigmoid(convs[0]) * convs[1]).astype(o_ref.dtype)
    for i, tail in enumerate(tails):
        carry_ref[i] = tail


def _mm(a, w, layer, *, mode, col_offs, n_cols, tn, tm, rc, seq_len, out_dtype=F32, transposed=False,
        gain=None, gain_layer=None, bias=None, conv_w=None, conv_b=None, res=None, name):
    t, k = a.shape
    nw = len(col_offs)
    norm = gain is not None
    assert not (norm and transposed)
    assert t % tm == 0 and seq_len % tm == 0 and tm % rc == 0 and all(off % tn == 0 for off in col_offs)
    n_tiles = pl.cdiv(n_cols, tn)
    out_cols = n_tiles * tn
    offs = [off // tn for off in col_offs]

    def col_spec(rows, ob):
        return pl.BlockSpec((None, rows, tn), lambda n, m: (layer, 0, n + ob))

    def weight_spec(ob):
        if transposed:
            return pl.BlockSpec((None, tn, k), lambda n, m: (layer, n + ob, 0))
        return col_spec(k, ob)

    in_specs = [pl.BlockSpec((tm, k), lambda n, m: (m, 0))]
    args = [a]
    if norm:
        in_specs.append(pl.BlockSpec((None, k, 1), lambda n, m: (gain_layer, 0, 0)))
        args.append(gain)
    for ob in offs:
        in_specs.append(weight_spec(ob))
        args.append(w)
    tile_spec = pl.BlockSpec((tm, tn), lambda n, m: (m, n))
    scratch = [pltpu.VMEM((k, tn), BF16) for _ in offs]

    if mode == "glu":
        in_specs += [col_spec(1, ob) for ob in offs]
        args += [bias, bias]
    elif mode == "res":
        in_specs.append(tile_spec)
        args.append(res)
    elif mode == "res_bias":
        in_specs += [tile_spec, col_spec(1, offs[0])]
        args += [res, bias]
    elif mode == "conv_silu":
        kc = conv_w.shape[1]
        in_specs += [col_spec(kc, 0), col_spec(1, 0)]
        args += [conv_w, conv_b]
    elif mode == "ffn":
        kc = conv_w.shape[1]
        in_specs += [col_spec(kc, ob) for ob in offs] + [col_spec(1, ob) for ob in offs]
        args += [conv_w, conv_w, conv_b, conv_b]
    if mode in ("conv_silu", "ffn"):
        assert kc - 1 <= SUBLANES
        scratch += [pltpu.VMEM((kc + 1, SUBLANES, tn), F32) for _ in offs]
        scratch += [pltpu.VMEM((nw, SUBLANES, tn), F32)]
    if norm:
        scratch += [pltpu.VMEM((t // tm, tm, LANES), F32)]

    kern = functools.partial(_mm_kernel, mode=mode, nw=nw, tiles_per_seq=seq_len // tm, n_valid=n_cols,
                             rc=rc, transposed=transposed, norm=norm)
    return pl.pallas_call(
        kern,
        out_shape=jax.ShapeDtypeStruct((t, out_cols), out_dtype),
        grid=(n_tiles, t // tm),
        in_specs=in_specs,
        out_specs=tile_spec,
        scratch_shapes=scratch,
        compiler_params=_params(("arbitrary" if norm else "parallel", "arbitrary")),
        name=name,
    )(*args)


def _conf_mid_kernel(cur_ref, halo_ref, w_ref, b_ref, g_ref, beta_ref, o_ref, ext_ref, conv_ref, w8_ref,
                     *, tm, tiles_per_seq, rc):
    kc = w_ref.shape[0]
    d = cur_ref.shape[1]
    seq_start = (pl.program_id(0) % tiles_per_seq) == 0

    @pl.when(seq_start)
    def _():
        ext_ref[0:CONV_HALO, :] = jnp.zeros((CONV_HALO, d), F32)

    @pl.when(jnp.logical_not(seq_start))
    def _():
        ext_ref[0:CONV_HALO, :] = halo_ref[...]

    ext_ref[CONV_HALO:CONV_HALO + tm, :] = cur_ref[...]
    for k in range(kc):
        w8_ref[k] = jnp.broadcast_to(w_ref[k:k + 1, :], (SUBLANES, d))

    base = CONV_HALO - (kc - 1)
    for cj in range(d // LANES):
        cols = slice(cj * LANES, (cj + 1) * LANES)

        def row_body(ri, carry, cols=cols):
            r0 = pl.multiple_of(ri * rc, rc)
            e = ext_ref[pl.ds(r0, rc + CONV_HALO), cols]
            acc = jnp.zeros((rc, LANES), F32) + b_ref[:, cols]
            for r in range(SUBLANES):
                taps = [k for k in range(kc) if (base + k) % SUBLANES == r]
                if not taps:
                    continue
                n = rc if r == 0 else rc + SUBLANES
                part = None
                for k in taps:
                    lo = (base + k) // SUBLANES * SUBLANES
                    term = e[lo:lo + n, :] * _tile_rows(w8_ref[k, :, cols], n)
                    part = term if part is None else part + term
                acc = acc + part[r:r + rc, :]
            conv_ref[pl.ds(r0, rc), cols] = acc
            return carry

        lax.fori_loop(0, tm // rc, row_body, 0)

    c = conv_ref[...]
    mu = jnp.mean(c, axis=-1, keepdims=True)
    xc = c - mu
    var = jnp.mean(xc * xc, axis=-1, keepdims=True)
    y = xc * lax.rsqrt(var + LN_EPS) * g_ref[...] + beta_ref[...]
    o_ref[...] = (y * jax.nn.sigmoid(y)).astype(o_ref.dtype)


def _conf_mid(v, w_dw, b_dw, ln_g, ln_b, layer, seq_len, tm):
    t, d = v.shape
    kc = w_dw.shape[1]
    assert kc - 1 <= CONV_HALO and seq_len % tm == 0 and tm % CONV_HALO == 0 and d % LANES == 0
    rc = min(128, tm)
    halo_per_tile = tm // CONV_HALO
    row = lambda m: (layer, 0, 0)
    kern = functools.partial(_conf_mid_kernel, tm=tm, tiles_per_seq=seq_len // tm, rc=rc)
    return pl.pallas_call(
        kern,
        out_shape=jax.ShapeDtypeStruct((t, d), BF16),
        grid=(t // tm,),
        in_specs=[pl.BlockSpec((tm, d), lambda m: (m, 0)),
                  pl.BlockSpec((CONV_HALO, d), lambda m: (jnp.maximum(m * halo_per_tile - 1, 0), 0)),
                  pl.BlockSpec((None, kc, d), row),
                  pl.BlockSpec((None, 1, d), row),
                  pl.BlockSpec((None, 1, d), row),
                  pl.BlockSpec((None, 1, d), row)],
        out_specs=pl.BlockSpec((tm, d), lambda m: (m, 0)),
        scratch_shapes=[pltpu.VMEM((CONV_HALO + tm, d), F32), pltpu.VMEM((tm, d), F32),
                        pltpu.VMEM((kc, SUBLANES, d), F32)],
        compiler_params=_params(("parallel",)),
        name="conformer_conv_ln_swish",
    )(v, v, w_dw, b_dw, ln_g, ln_b)


def _split2(x):
    hi = x.astype(BF16)
    lo = (x - hi.astype(F32)).astype(BF16)
    return hi, lo


def _split3(x):
    hi, lo = _split2(x)
    lo2 = (x - hi.astype(F32) - lo.astype(F32)).astype(BF16)
    return hi, lo, lo2


def _ssd_kernel(x_ref, b_ref, c_ref, z_ref, dt_ref, dtb_ref, alog_ref, dskip_ref, ng_ref, o_ref,
                state_ref, parts_ref, dte_ref, acse_ref, *, tl, gw, gs):
    q = SSD_CHUNK
    lt = pl.program_id(1)
    g0 = pl.program_id(2) * gs
    hpg = gw // SSD_HEAD_DIM

    @pl.when(lt == 0)
    def _():
        for gi in range(gs):
            state_ref[g0 + gi] = jnp.zeros(state_ref.shape[1:], F32)

    @pl.when(g0 == 0)
    def _():
        pre = dt_ref[...] + dtb_ref[...]
        dt = jnp.maximum(pre, 0.0) + jnp.log1p(jnp.exp(-jnp.abs(pre)))
        a = dt * (-jnp.exp(alog_ref[...]) * LOG2_E)
        r = lax.broadcasted_iota(jnp.int32, (tl, tl), 0)
        c = lax.broadcasted_iota(jnp.int32, (tl, tl), 1)
        tri = jnp.where((r >= c) & ((r >> Q_SHIFT) == (c >> Q_SHIFT)), 1.0, 0.0).astype(BF16)
        acs = jnp.zeros((tl, LANES), F32)
        for part in _split3(a):
            acs = acs + jnp.dot(tri, part, preferred_element_type=F32)
        for i, part in enumerate(_split2(dt) + _split2(acs)):
            parts_ref[i * tl:(i + 1) * tl, :] = part

    hrow = lax.broadcasted_iota(jnp.int32, (LANES, gs * gw), 0)
    hcol = lax.broadcasted_iota(jnp.int32, (LANES, gs * gw), 1)
    expand = jnp.where(hrow == g0 * hpg + (hcol >> HEAD_SHIFT), 1.0, 0.0).astype(BF16)
    ex = jnp.dot(parts_ref[...], expand, preferred_element_type=F32)
    dte_ref[...] = ex[0:tl] + ex[tl:2 * tl]
    acse_ref[...] = ex[2 * tl:3 * tl] + ex[3 * tl:4 * tl]

    row = lax.broadcasted_iota(jnp.int32, (q, gw), 0)
    pos = lax.broadcasted_iota(jnp.int32, (q, gw), 1) & (SSD_HEAD_DIM - 1)
    causal = row >= pos
    diagonal = row == pos
    br = lax.broadcasted_iota(jnp.int32, (MXU_DIM, MXU_DIM), 0) >> HEAD_SHIFT
    bc = lax.broadcasted_iota(jnp.int32, (MXU_DIM, MXU_DIM), 1) >> HEAD_SHIFT
    head_mask = jnp.where(br == bc, 1.0, 0.0).astype(BF16)
    heads_per_dot = MXU_DIM // SSD_HEAD_DIM

    for ci in range(tl // q):
        rows = pl.ds(ci * q, q)
        for gi in range(gs):
            lanes = slice(gi * gw, (gi + 1) * gw)
            nlanes = slice(gi * SSD_D_STATE, (gi + 1) * SSD_D_STATE)
            x = x_ref[rows, lanes]
            bm = b_ref[rows, nlanes].astype(BF16)
            cm = c_ref[rows, nlanes].astype(BF16)
            acs = acse_ref[rows, lanes]
            a_last = acse_ref[pl.ds(ci * q + q - 1, 1), lanes]
            xd = x * dte_ref[rows, lanes]
            xd_bf = xd.astype(BF16)

            prev = state_ref[g0 + gi]
            y = jnp.dot(cm, prev.astype(BF16), preferred_element_type=F32) * jnp.exp2(acs)

            wgt = (jnp.exp2(a_last - acs) * xd).astype(BF16)
            s_new = lax.dot_general(bm, wgt, (((0,), (0,)), ((), ())), preferred_element_type=F32)
            state_ref[g0 + gi] = prev * jnp.exp2(a_last) + s_new

            cb = lax.dot_general(cm, jnp.concatenate([bm] * hpg, axis=0), (((1,), (1,)), ((), ())),
                                 preferred_element_type=F32)
            acs_s = jnp.sum(jnp.where(diagonal, acs, 0.0), axis=0, keepdims=True)
            decay = jnp.exp2(jnp.where(causal, acs - acs_s, NEG_BIG))
            mix = (cb * decay).astype(BF16)
            parts = []
            for j in range(gw // MXU_DIM):
                cols = slice(j * MXU_DIM, (j + 1) * MXU_DIM)
                blockdiag = jnp.concatenate([xd_bf[:, cols]] * heads_per_dot, axis=0) * head_mask
                parts.append(jnp.dot(mix[:, cols], blockdiag, preferred_element_type=F32))
            y = y + jnp.concatenate(parts, axis=1) + dskip_ref[:, lanes] * x

            z = z_ref[rows, lanes]
            yg = y * (z * jax.nn.sigmoid(z))
            ms = jnp.mean(yg * yg, axis=-1, keepdims=True)
            o_ref[rows, lanes] = (yg * lax.rsqrt(ms + RMS_EPS) * ng_ref[:, lanes]).astype(o_ref.dtype)


def _ssd(xbc, z, dt, dt_bias, a_log, d_skip, norm_g, layer, batch, seq_len, tl, gs):
    t, d_inner = z.shape
    gw = d_inner // SSD_N_GROUPS
    bw = gs * SSD_D_STATE
    assert gw % MXU_DIM == 0 and SSD_D_STATE == LANES and seq_len % tl == 0 and tl % SSD_CHUNK == 0
    assert SSD_N_GROUPS % gs == 0 and d_inner % bw == 0
    nl = seq_len // tl
    b_blk = d_inner // bw
    c_blk = b_blk + SSD_N_GROUPS // gs
    rows = lambda b, l, g: (b * nl + l, g)
    head_row = lambda b, l, g: (layer, 0, 0)
    group_row = lambda b, l, g: (layer, 0, g)
    kern = functools.partial(_ssd_kernel, tl=tl, gw=gw, gs=gs)
    return pl.pallas_call(
        kern,
        out_shape=jax.ShapeDtypeStruct((t, d_inner), BF16),
        grid=(batch, nl, SSD_N_GROUPS // gs),
        in_specs=[pl.BlockSpec((tl, gs * gw), rows),
                  pl.BlockSpec((tl, bw), lambda b, l, g: (b * nl + l, b_blk + g)),
                  pl.BlockSpec((tl, bw), lambda b, l, g: (b * nl + l, c_blk + g)),
                  pl.BlockSpec((tl, gs * gw), rows),
                  pl.BlockSpec((tl, LANES), lambda b, l, g: (b * nl + l, 0)),
                  pl.BlockSpec((None, 1, LANES), head_row),
                  pl.BlockSpec((None, 1, LANES), head_row),
                  pl.BlockSpec((None, 1, gs * gw), group_row),
                  pl.BlockSpec((None, 1, gs * gw), group_row)],
        out_specs=pl.BlockSpec((tl, gs * gw), rows),
        scratch_shapes=[pltpu.VMEM((SSD_N_GROUPS, SSD_D_STATE, gw), F32),
                        pltpu.VMEM((4 * tl, LANES), BF16),
                        pltpu.VMEM((tl, gs * gw), F32),
                        pltpu.VMEM((tl, gs * gw), F32)],
        compiler_params=_params(("parallel", "arbitrary", "arbitrary")),
        name="ssd_scan_gate_norm",
    )(xbc, xbc, xbc, z, dt, dt_bias, a_log, d_skip, norm_g)


def _row3(p):
    return p.reshape(p.shape[0], 1, p.shape[1])


def _pad_lanes(p):
    return jnp.pad(p, ((0, 0), (0, LANES - p.shape[1])))


def kernel(x, norm_mix_g, norm_ffn_g, norm_final_g, cv_w_in, cv_b_in, cv_w_dw, cv_b_dw, cv_ln_g, cv_ln_b, cv_w_out, cv_b_out, ssm_w_in, ssm_w_conv, ssm_b_conv, ssm_dt_bias, ssm_a_log, ssm_d, ssm_norm_g, ssm_w_out, ffn_w_up, ffn_w_dw, ffn_b_dw, ffn_w_down):
    batch, seq_len, d = x.shape
    t = batch * seq_len
    depth = norm_mix_g.shape[0]
    d_inner = ssm_w_out.shape[1]
    n_heads = ssm_dt_bias.shape[1]
    gn = SSD_N_GROUPS * SSD_D_STATE
    f = ffn_w_down.shape[1]
    assert n_heads * SSD_HEAD_DIM == d_inner and n_heads <= LANES
    assert ssm_w_in.shape[2] == 2 * d_inner + 2 * gn + n_heads

    tm = min(1024, seq_len)
    tm2 = min(2048, seq_len)
    tm_k = min(512, seq_len)
    tm_s = min(256, seq_len)
    tn = min(512, d)
    tn2 = min(1024, d)
    tnorm = min(1024, seq_len)
    tconv = min(512, seq_len)
    tl = min(512, seq_len)
    rc = min(256, seq_len)
    rc_t = min(512, seq_len)

    mix_gain = norm_mix_g.reshape(depth, d, 1)
    ffn_gain = norm_ffn_g.reshape(depth, d, 1)
    norm_mix_g = _row3(norm_mix_g)
    norm_final_g = norm_final_g.reshape(1, 1, d)
    cv_b_in, cv_b_dw, cv_ln_g, cv_ln_b, cv_b_out = map(_row3, (cv_b_in, cv_b_dw, cv_ln_g, cv_ln_b, cv_b_out))
    ssm_b_conv, ssm_norm_g, ffn_b_dw = map(_row3, (ssm_b_conv, ssm_norm_g, ffn_b_dw))
    dt_bias = _row3(_pad_lanes(ssm_dt_bias))
    a_log = _row3(_pad_lanes(ssm_a_log))
    d_skip = _row3(jnp.repeat(ssm_d, SSD_HEAD_DIM, axis=1))
    ssm_w_in_t = jnp.swapaxes(ssm_w_in, 1, 2)

    xf = x.reshape(t, d)
    for i in range(depth):
        j = i // 2
        if i % 2 == 0:
            v = _mm(xf, cv_w_in, j, mode="glu", col_offs=(0, d), n_cols=d, tn=tn, tm=tm, rc=rc, seq_len=seq_len,
                    gain=mix_gain, gain_layer=i, bias=cv_b_in, name="conformer_in_glu")
            s = _conf_mid(v, cv_w_dw, cv_b_dw, cv_ln_g, cv_ln_b, j, seq_len, tconv)
            xf = _mm(s, cv_w_out, j, mode="res_bias", col_offs=(0,), n_cols=d, tn=tn2, tm=tm, rc=rc,
                     seq_len=seq_len, res=xf, bias=cv_b_out, name="conformer_out")
        else:
            h = _rmsnorm(xf, norm_mix_g, i, BF16, tnorm)
            z = _mm(h, ssm_w_in_t, j, mode="plain", col_offs=(0,), n_cols=d_inner, tn=tn2, tm=tm, rc=rc_t,
                    seq_len=seq_len, transposed=True, name="ssm_in_z")
            xbc = _mm(h, ssm_w_in_t, j, mode="conv_silu", col_offs=(d_inner,), n_cols=d_inner + 2 * gn, tn=tn2,
                      tm=tm, rc=rc, seq_len=seq_len, transposed=True, conv_w=ssm_w_conv, conv_b=ssm_b_conv,
                      name="ssm_in_xbc")
            dt = _mm(h, ssm_w_in_t, j, mode="mask_cols", col_offs=(2 * d_inner + 2 * gn,), n_cols=n_heads,
                     tn=LANES, tm=tm, rc=rc_t, seq_len=seq_len, transposed=True, name="ssm_in_dt")
            yn = _ssd(xbc, z, dt, dt_bias, a_log, d_skip, ssm_norm_g, j, batch, seq_len, tl, SSD_GROUPS_PER_STEP)
            xf = _mm(yn, ssm_w_out, j, mode="res", col_offs=(0,), n_cols=d, tn=tn2, tm=tm_s, rc=tm_s,
                     seq_len=seq_len, res=xf, name="ssm_out")
        act = _mm(xf, ffn_w_up, i, mode="ffn", col_offs=(0, f), n_cols=f, tn=tn, tm=tm, rc=rc, seq_len=seq_len,
                  out_dtype=BF16, gain=ffn_gain, gain_layer=i, conv_w=ffn_w_dw, conv_b=ffn_b_dw,
                  name="ffn_up_conv_gate")
        xf = _mm(act, ffn_w_down, i, mode="res", col_offs=(0,), n_cols=d, tn=tn, tm=tm_k, rc=tm_k, seq_len=seq_len,
                 res=xf, name="ffn_down")
    out = _rmsnorm(xf, norm_final_g, 0, F32, tnorm)
    return out.reshape(batch, seq_len, d)
```

```python
import functools

import jax
import jax.numpy as jnp
from jax import lax
from jax.experimental import pallas as pl
from jax.experimental.pallas import tpu as pltpu

RMS_EPS = 1e-6
LN_EPS = 1e-5

SSD_CHUNK = 64
SSD_HEAD_DIM = 64
SSD_N_GROUPS = 8
SSD_D_STATE = 128
SSD_GROUPS_PER_STEP = 4
Q_SHIFT = SSD_CHUNK.bit_length() - 1
HEAD_SHIFT = SSD_HEAD_DIM.bit_length() - 1

LANES = 128
SUBLANES = 8
MXU_DIM = 256
VMEM_LIMIT_BYTES = 56 * 1024 * 1024

CONV_HALO = 32
NEG_BIG = -1e30
LOG2_E = 1.4426950408889634

F32 = jnp.float32
BF16 = jnp.bfloat16


def _params(semantics):
    return pltpu.CompilerParams(dimension_semantics=semantics, vmem_limit_bytes=VMEM_LIMIT_BYTES)


def _rmsnorm_kernel(x_ref, g_ref, o_ref):
    x = x_ref[...]
    ms = jnp.mean(x * x, axis=-1, keepdims=True)
    o_ref[...] = (x * lax.rsqrt(ms + RMS_EPS) * g_ref[...]).astype(o_ref.dtype)


def _rmsnorm(x, gains, layer, out_dtype, tm):
    t, d = x.shape
    return pl.pallas_call(
        _rmsnorm_kernel,
        out_shape=jax.ShapeDtypeStruct((t, d), out_dtype),
        grid=(t // tm,),
        in_specs=[pl.BlockSpec((tm, d), lambda m: (m, 0)),
                  pl.BlockSpec((None, 1, d), lambda m: (layer, 0, 0))],
        out_specs=pl.BlockSpec((tm, d), lambda m: (m, 0)),
        compiler_params=_params(("parallel",)),
        name="rmsnorm",
    )(x, gains)


def _tile_rows(x8, rows):
    return jnp.tile(x8, (rows // SUBLANES, 1))


def _causal_conv_rows(u, tail, p8_ref, kc):
    rc = u.shape[0]
    ext = jnp.concatenate([tail, u], axis=0)
    out = u * _tile_rows(p8_ref[kc - 1], rc) + _tile_rows(p8_ref[kc], rc)
    for k in range(kc - 1):
        start = SUBLANES - (kc - 1 - k)
        out = out + ext[start:start + rc, :] * _tile_rows(p8_ref[k], rc)
    return out


def _mm_kernel(*refs, mode, nw, tiles_per_seq, n_valid, rc, transposed, norm):
    a_ref = refs[0]
    refs = refs[1:]
    if norm:
        gain_ref, rstd_ref = refs[0], refs[-1]
        refs = refs[1:-1]
    w_refs = refs[:nw]
    rest = refs[nw:]
    m = pl.program_id(1)
    tm = a_ref.shape[0]
    conv_refs, p8_refs, carry_ref = (), (), None

    if mode == "glu":
        ba_ref, bg_ref, o_ref, *wb_refs = rest
    elif mode == "res":
        res_ref, o_ref, *wb_refs = rest
    elif mode == "res_bias":
        res_ref, b_ref, o_ref, *wb_refs = rest
    elif mode == "plain" or mode == "mask_cols":
        o_ref, *wb_refs = rest
    elif mode == "conv_silu":
        cw_ref, cb_ref, o_ref, wb0, p8, carry_ref = rest
        wb_refs, conv_refs, p8_refs = [wb0], [(cw_ref, cb_ref)], [p8]
    elif mode == "ffn":
        cwg_ref, cwv_ref, cbg_ref, cbv_ref, o_ref, wb0, wb1, p8g, p8v, carry_ref = rest
        wb_refs, conv_refs, p8_refs = [wb0, wb1], [(cwg_ref, cbg_ref), (cwv_ref, cbv_ref)], [p8g, p8v]
    else:
        raise ValueError(mode)
    kc = conv_refs[0][0].shape[0] if conv_refs else 0

    @pl.when(m == 0)
    def _():
        for w_ref, wb_ref in zip(w_refs, wb_refs):
            w = w_ref[...] * gain_ref[...] if norm else w_ref[...]
            wb_ref[...] = (w.T if transposed else w).astype(BF16)
        for (cw_ref, cb_ref), p8_ref in zip(conv_refs, p8_refs):
            for k in range(kc):
                p8_ref[k] = jnp.broadcast_to(cw_ref[k:k + 1, :], p8_ref.shape[1:])
            p8_ref[kc] = jnp.broadcast_to(cb_ref[...], p8_ref.shape[1:])

    if conv_refs:
        @pl.when((m % tiles_per_seq) == 0)
        def _():
            carry_ref[...] = jnp.zeros(carry_ref.shape, F32)

    if norm:
        @pl.when(pl.program_id(0) == 0)
        def _():
            x = a_ref[...]
            ms = jnp.mean(x * x, axis=-1, keepdims=True)
            rstd_ref[m] = jnp.broadcast_to(lax.rsqrt(ms + RMS_EPS), rstd_ref.shape[1:])

    def chunk_matmuls(c):
        rows = slice(c * rc, (c + 1) * rc)
        a = a_ref[rows, :].astype(BF16)
        accs = [jnp.dot(a, wb_ref[...], preferred_element_type=F32) for wb_ref in wb_refs]
        if norm:
            scale = jnp.tile(rstd_ref[m, rows, :], (1, accs[0].shape[1] // LANES))
            accs = [acc * scale for acc in accs]
        return accs

    tails = [carry_ref[i] for i in range(len(conv_refs))]
    n_chunks = tm // rc
    accs_next = chunk_matmuls(0)
    for c in range(n_chunks):
        rows = slice(c * rc, (c + 1) * rc)
        accs = accs_next
        if c + 1 < n_chunks:
            accs_next = chunk_matmuls(c + 1)
        if mode == "glu":
            o_ref[rows, :] = (accs[0] + ba_ref[...]) * jax.nn.sigmoid(accs[1] + bg_ref[...])
        elif mode == "res":
            o_ref[rows, :] = res_ref[rows, :] + accs[0]
        elif mode == "res_bias":
            o_ref[rows, :] = res_ref[rows, :] + (accs[0] + b_ref[...])
        elif mode == "plain":
            o_ref[rows, :] = accs[0].astype(o_ref.dtype)
        elif mode == "mask_cols":
            lane = lax.broadcasted_iota(jnp.int32, accs[0].shape, 1)
            o_ref[rows, :] = jnp.where(lane < n_valid, accs[0], 0.0)
        else:
            convs = [_causal_conv_rows(u, tail, p8_ref, kc) for u, tail, p8_ref in zip(accs, tails, p8_refs)]
            tails = [u[rc - SUBLANES:rc, :] for u in accs]
            if mode == "conv_silu":
                o_ref[rows, :] = convs[0] * jax.nn.sigmoid(convs[0])
            else:
                o_ref[rows, :] = (convs[0] * jax.nn.sigmoid(convs[0]) * convs[1]).astype(o_ref.dtype)
    for i, tail in enumerate(tails):
        carry_ref[i] = tail


def _mm(a, w, layer, *, mode, col_offs, n_cols, tn, tm, rc, seq_len, out_dtype=F32, transposed=False,
        gain=None, gain_layer=None, bias=None, conv_w=None, conv_b=None, res=None, name):
    t, k = a.shape
    nw = len(col_offs)
    norm = gain is not None
    assert not (norm and transposed)
    assert t % tm == 0 and seq_len % tm == 0 and tm % rc == 0 and all(off % tn == 0 for off in col_offs)
    n_tiles = pl.cdiv(n_cols, tn)
    out_cols = n_tiles * tn
    offs = [off // tn for off in col_offs]

    def col_spec(rows, ob):
        return pl.BlockSpec((None, rows, tn), lambda n, m: (layer, 0, n + ob))

    def weight_spec(ob):
        if transposed:
            return pl.BlockSpec((None, tn, k), lambda n, m: (layer, n + ob, 0))
        return col_spec(k, ob)

    in_specs = [pl.BlockSpec((tm, k), lambda n, m: (m, 0))]
    args = [a]
    if norm:
        in_specs.append(pl.BlockSpec((None, k, 1), lambda n, m: (gain_layer, 0, 0)))
        args.append(gain)
    for ob in offs:
        in_specs.append(weight_spec(ob))
        args.append(w)
    tile_spec = pl.BlockSpec((tm, tn), lambda n, m: (m, n))
    scratch = [pltpu.VMEM((k, tn), BF16) for _ in offs]

    if mode == "glu":
        in_specs += [col_spec(1, ob) for ob in offs]
        args += [bias, bias]
    elif mode == "res":
        in_specs.append(tile_spec)
        args.append(res)
    elif mode == "res_bias":
        in_specs += [tile_spec, col_spec(1, offs[0])]
        args += [res, bias]
    elif mode == "conv_silu":
        kc = conv_w.shape[1]
        in_specs += [col_spec(kc, 0), col_spec(1, 0)]
        args += [conv_w, conv_b]
    elif mode == "ffn":
        kc = conv_w.shape[1]
        in_specs += [col_spec(kc, ob) for ob in offs] + [col_spec(1, ob) for ob in offs]
        args += [conv_w, conv_w, conv_b, conv_b]
    if mode in ("conv_silu", "ffn"):
        assert kc - 1 <= SUBLANES
        scratch += [pltpu.VMEM((kc + 1, SUBLANES, tn), F32) for _ in offs]
        scratch += [pltpu.VMEM((nw, SUBLANES, tn), F32)]
    if norm:
        scratch += [pltpu.VMEM((t // tm, tm, LANES), F32)]

    kern = functools.partial(_mm_kernel, mode=mode, nw=nw, tiles_per_seq=seq_len // tm, n_valid=n_cols,
                             rc=rc, transposed=transposed, norm=norm)
    return pl.pallas_call(
        kern,
        out_shape=jax.ShapeDtypeStruct((t, out_cols), out_dtype),
        grid=(n_tiles, t // tm),
        in_specs=in_specs,
        out_specs=tile_spec,
        scratch_shapes=scratch,
        compiler_params=_params(("arbitrary" if norm else "parallel", "arbitrary")),
        name=name,
    )(*args)


def _conf_mid_kernel(cur_ref, halo_ref, w_ref, b_ref, g_ref, beta_ref, o_ref, ext_ref, conv_ref, w8_ref,
                     *, tm, tiles_per_seq, rc):
    kc = w_ref.shape[0]
    d = cur_ref.shape[1]
    seq_start = (pl.program_id(0) % tiles_per_seq) == 0

    @pl.when(seq_start)
    def _():
        ext_ref[0:CONV_HALO, :] = jnp.zeros((CONV_HALO, d), F32)

    @pl.when(jnp.logical_not(seq_start))
    def _():
        ext_ref[0:CONV_HALO, :] = halo_ref[...]

    ext_ref[CONV_HALO:CONV_HALO + tm, :] = cur_ref[...]
    for k in range(kc):
        w8_ref[k] = jnp.broadcast_to(w_ref[k:k + 1, :], (SUBLANES, d))

    base = CONV_HALO - (kc - 1)
    for cj in range(d // LANES):
        cols = slice(cj * LANES, (cj + 1) * LANES)

        def row_body(ri, carry, cols=cols):
            r0 = pl.multiple_of(ri * rc, rc)
            e = ext_ref[pl.ds(r0, rc + CONV_HALO), cols]
            acc = jnp.zeros((rc, LANES), F32) + b_ref[:, cols]
            for r in range(SUBLANES):
                taps = [k for k in range(kc) if (base + k) % SUBLANES == r]
                if not taps:
                    continue
                n = rc if r == 0 else rc + SUBLANES
                part = None
                for k in taps:
                    lo = (base + k) // SUBLANES * SUBLANES
                    term = e[lo:lo + n, :] * _tile_rows(w8_ref[k, :, cols], n)
                    part = term if part is None else part + term
                acc = acc + part[r:r + rc, :]
            conv_ref[pl.ds(r0, rc), cols] = acc
            return carry

        lax.fori_loop(0, tm // rc, row_body, 0)

    c = conv_ref[...]
    mu = jnp.mean(c, axis=-1, keepdims=True)
    xc = c - mu
    var = jnp.mean(xc * xc, axis=-1, keepdims=True)
    y = xc * lax.rsqrt(var + LN_EPS) * g_ref[...] + beta_ref[...]
    o_ref[...] = (y * jax.nn.sigmoid(y)).astype(o_ref.dtype)


def _conf_mid(v, w_dw, b_dw, ln_g, ln_b, layer, seq_len, tm):
    t, d = v.shape
    kc = w_dw.shape[1]
    assert kc - 1 <= CONV_HALO and seq_len % tm == 0 and tm % CONV_HALO == 0 and d % LANES == 0
    rc = min(128, tm)
    halo_per_tile = tm // CONV_HALO
    row = lambda m: (layer, 0, 0)
    kern = functools.partial(_conf_mid_kernel, tm=tm, tiles_per_seq=seq_len // tm, rc=rc)
    return pl.pallas_call(
        kern,
        out_shape=jax.ShapeDtypeStruct((t, d), BF16),
        grid=(t // tm,),
        in_specs=[pl.BlockSpec((tm, d), lambda m: (m, 0)),
                  pl.BlockSpec((CONV_HALO, d), lambda m: (jnp.maximum(m * halo_per_tile - 1, 0), 0)),
                  pl.BlockSpec((None, kc, d), row),
                  pl.BlockSpec((None, 1, d), row),
                  pl.BlockSpec((None, 1, d), row),
                  pl.BlockSpec((None, 1, d), row)],
        out_specs=pl.BlockSpec((tm, d), lambda m: (m, 0)),
        scratch_shapes=[pltpu.VMEM((CONV_HALO + tm, d), F32), pltpu.VMEM((tm, d), F32),
                        pltpu.VMEM((kc, SUBLANES, d), F32)],
        compiler_params=_params(("parallel",)),
        name="conformer_conv_ln_swish",
    )(v, v, w_dw, b_dw, ln_g, ln_b)


def _split2(x):
    hi = x.astype(BF16)
    lo = (x - hi.astype(F32)).astype(BF16)
    return hi, lo


def _split3(x):
    hi, lo = _split2(x)
    lo2 = (x - hi.astype(F32) - lo.astype(F32)).astype(BF16)
    return hi, lo, lo2


def _ssd_kernel(x_ref, b_ref, c_ref, z_ref, dt_ref, dtb_ref, alog_ref, dskip_ref, ng_ref, o_ref,
                state_ref, parts_ref, dte_ref, acse_ref, *, tl, gw, gs):
    q = SSD_CHUNK
    lt = pl.program_id(1)
    g0 = pl.program_id(2) * gs
    hpg = gw // SSD_HEAD_DIM

    @pl.when(lt == 0)
    def _():
        for gi in range(gs):
            state_ref[g0 + gi] = jnp.zeros(state_ref.shape[1:], F32)

    @pl.when(g0 == 0)
    def _():
        pre = dt_ref[...] + dtb_ref[...]
        dt = jnp.maximum(pre, 0.0) + jnp.log1p(jnp.exp(-jnp.abs(pre)))
        a = dt * (-jnp.exp(alog_ref[...]) * LOG2_E)
        r = lax.broadcasted_iota(jnp.int32, (tl, tl), 0)
        c = lax.broadcasted_iota(jnp.int32, (tl, tl), 1)
        tri = jnp.where((r >= c) & ((r >> Q_SHIFT) == (c >> Q_SHIFT)), 1.0, 0.0).astype(BF16)
        acs = jnp.zeros((tl, LANES), F32)
        for part in _split3(a):
            acs = acs + jnp.dot(tri, part, preferred_element_type=F32)
        for i, part in enumerate(_split2(dt) + _split2(acs)):
            parts_ref[i * tl:(i + 1) * tl, :] = part

    hrow = lax.broadcasted_iota(jnp.int32, (LANES, gs * gw), 0)
    hcol = lax.broadcasted_iota(jnp.int32, (LANES, gs * gw), 1)
    expand = jnp.where(hrow == g0 * hpg + (hcol >> HEAD_SHIFT), 1.0, 0.0).astype(BF16)
    ex = jnp.dot(parts_ref[...], expand, preferred_element_type=F32)
    dte_ref[...] = ex[0:tl] + ex[tl:2 * tl]
    acse_ref[...] = ex[2 * tl:3 * tl] + ex[3 * tl:4 * tl]

    row = lax.broadcasted_iota(jnp.int32, (q, gw), 0)
    pos = lax.broadcasted_iota(jnp.int32, (q, gw), 1) & (SSD_HEAD_DIM - 1)
    causal = row >= pos
    diagonal = row == pos
    br = lax.broadcasted_iota(jnp.int32, (MXU_DIM, MXU_DIM), 0) >> HEAD_SHIFT
    bc = lax.broadcasted_iota(jnp.int32, (MXU_DIM, MXU_DIM), 1) >> HEAD_SHIFT
    head_mask = jnp.where(br == bc, 1.0, 0.0).astype(BF16)
    heads_per_dot = MXU_DIM // SSD_HEAD_DIM

    for ci in range(tl // q):
        rows = pl.ds(ci * q, q)
        for gi in range(gs):
            lanes = slice(gi * gw, (gi + 1) * gw)
            nlanes = slice(gi * SSD_D_STATE, (gi + 1) * SSD_D_STATE)
            x = x_ref[rows, lanes]
            bm = b_ref[rows, nlanes].astype(BF16)
            cm = c_ref[rows, nlanes].astype(BF16)
            acs = acse_ref[rows, lanes]
            a_last = acse_ref[pl.ds(ci * q + q - 1, 1), lanes]
            xd = x * dte_ref[rows, lanes]
            xd_bf = xd.astype(BF16)

            prev = state_ref[g0 + gi]
            y = jnp.dot(cm, prev.astype(BF16), preferred_element_type=F32) * jnp.exp2(acs)

            wgt = (jnp.exp2(a_last - acs) * xd).astype(BF16)
            s_new = lax.dot_general(bm, wgt, (((0,), (0,)), ((), ())), preferred_element_type=F32)
            state_ref[g0 + gi] = prev * jnp.exp2(a_last) + s_new

            cb = lax.dot_general(cm, jnp.concatenate([bm] * hpg, axis=0), (((1,), (1,)), ((), ())),
                                 preferred_element_type=F32)
            acs_s = jnp.sum(jnp.where(diagonal, acs, 0.0), axis=0, keepdims=True)
            decay = jnp.exp2(jnp.where(causal, acs - acs_s, NEG_BIG))
            mix = (cb * decay).astype(BF16)
            parts = []
            for j in range(gw // MXU_DIM):
                cols = slice(j * MXU_DIM, (j + 1) * MXU_DIM)
                blockdiag = jnp.concatenate([xd_bf[:, cols]] * heads_per_dot, axis=0) * head_mask
                parts.append(jnp.dot(mix[:, cols], blockdiag, preferred_element_type=F32))
            y = y + jnp.concatenate(parts, axis=1) + dskip_ref[:, lanes] * x

            z = z_ref[rows, lanes]
            yg = y * (z * jax.nn.sigmoid(z))
            ms = jnp.mean(yg * yg, axis=-1, keepdims=True)
            o_ref[rows, lanes] = (yg * lax.rsqrt(ms + RMS_EPS) * ng_ref[:, lanes]).astype(o_ref.dtype)


def _ssd(xbc, z, dt, dt_bias, a_log, d_skip, norm_g, layer, batch, seq_len, tl, gs):
    t, d_inner = z.shape
    gw = d_inner // SSD_N_GROUPS
    bw = gs * SSD_D_STATE
    assert gw % MXU_DIM == 0 and SSD_D_STATE == LANES and seq_len % tl == 0 and tl % SSD_CHUNK == 0
    assert SSD_N_GROUPS % gs == 0 and d_inner % bw == 0
    nl = seq_len // tl
    b_blk = d_inner // bw
    c_blk = b_blk + SSD_N_GROUPS // gs
    rows = lambda b, l, g: (b * nl + l, g)
    head_row = lambda b, l, g: (layer, 0, 0)
    group_row = lambda b, l, g: (layer, 0, g)
    kern = functools.partial(_ssd_kernel, tl=tl, gw=gw, gs=gs)
    return pl.pallas_call(
        kern,
        out_shape=jax.ShapeDtypeStruct((t, d_inner), BF16),
        grid=(batch, nl, SSD_N_GROUPS // gs),
        in_specs=[pl.BlockSpec((tl, gs * gw), rows),
                  pl.BlockSpec((tl, bw), lambda b, l, g: (b * nl + l, b_blk + g)),
                  pl.BlockSpec((tl, bw), lambda b, l, g: (b * nl + l, c_blk + g)),
                  pl.BlockSpec((tl, gs * gw), rows),
                  pl.BlockSpec((tl, LANES), lambda b, l, g: (b * nl + l, 0)),
                  pl.BlockSpec((None, 1, LANES), head_row),
                  pl.BlockSpec((None, 1, LANES), head_row),
                  pl.BlockSpec((None, 1, gs * gw), group_row),
                  pl.BlockSpec((None, 1, gs * gw), group_row)],
        out_specs=pl.BlockSpec((tl, gs * gw), rows),
        scratch_shapes=[pltpu.VMEM((SSD_N_GROUPS, SSD_D_STATE, gw), F32),
                        pltpu.VMEM((4 * tl, LANES), BF16),
                        pltpu.VMEM((tl, gs * gw), F32),
                        pltpu.VMEM((tl, gs * gw), F32)],
        compiler_params=_params(("parallel", "arbitrary", "arbitrary")),
        name="ssd_scan_gate_norm",
    )(xbc, xbc, xbc, z, dt, dt_bias, a_log, d_skip, norm_g)


def _row3(p):
    return p.reshape(p.shape[0], 1, p.shape[1])


def _pad_lanes(p):
    return jnp.pad(p, ((0, 0), (0, LANES - p.shape[1])))


def kernel(x, norm_mix_g, norm_ffn_g, norm_final_g, cv_w_in, cv_b_in, cv_w_dw, cv_b_dw, cv_ln_g, cv_ln_b, cv_w_out, cv_b_out, ssm_w_in, ssm_w_conv, ssm_b_conv, ssm_dt_bias, ssm_a_log, ssm_d, ssm_norm_g, ssm_w_out, ffn_w_up, ffn_w_dw, ffn_b_dw, ffn_w_down):
    batch, seq_len, d = x.shape
    t = batch * seq_len
    depth = norm_mix_g.shape[0]
    d_inner = ssm_w_out.shape[1]
    n_heads = ssm_dt_bias.shape[1]
    gn = SSD_N_GROUPS * SSD_D_STATE
    f = ffn_w_down.shape[1]
    assert n_heads * SSD_HEAD_DIM == d_inner and n_heads <= LANES
    assert ssm_w_in.shape[2] == 2 * d_inner + 2 * gn + n_heads

    tm = min(1024, seq_len)
    tm_k = min(512, seq_len)
    tm_s = min(256, seq_len)
    tn = min(512, d)
    tn2 = min(1024, d)
    tnorm = min(1024, seq_len)
    tconv = min(512, seq_len)
    tl = min(512, seq_len)
    rc = min(256, seq_len)
    rc_t = min(512, seq_len)

    mix_gain = norm_mix_g.reshape(depth, d, 1)
    ffn_gain = norm_ffn_g.reshape(depth, d, 1)
    norm_mix_g = _row3(norm_mix_g)
    norm_final_g = norm_final_g.reshape(1, 1, d)
    cv_b_in, cv_b_dw, cv_ln_g, cv_ln_b, cv_b_out = map(_row3, (cv_b_in, cv_b_dw, cv_ln_g, cv_ln_b, cv_b_out))
    ssm_b_conv, ssm_norm_g, ffn_b_dw = map(_row3, (ssm_b_conv, ssm_norm_g, ffn_b_dw))
    dt_bias = _row3(_pad_lanes(ssm_dt_bias))
    a_log = _row3(_pad_lanes(ssm_a_log))
    d_skip = _row3(jnp.repeat(ssm_d, SSD_HEAD_DIM, axis=1))
    ssm_w_in_t = jnp.swapaxes(ssm_w_in, 1, 2)

    xf = x.reshape(t, d)
    for i in range(depth):
        j = i // 2
        if i % 2 == 0:
            v = _mm(xf, cv_w_in, j, mode="glu", col_offs=(0, d), n_cols=d, tn=tn, tm=tm, rc=rc, seq_len=seq_len,
                    gain=mix_gain, gain_layer=i, bias=cv_b_in, name="conformer_in_glu")
            s = _conf_mid(v, cv_w_dw, cv_b_dw, cv_ln_g, cv_ln_b, j, seq_len, tconv)
            xf = _mm(s, cv_w_out, j, mode="res_bias", col_offs=(0,), n_cols=d, tn=tn2, tm=tm, rc=rc,
                     seq_len=seq_len, res=xf, bias=cv_b_out, name="conformer_out")
        else:
            h = _rmsnorm(xf, norm_mix_g, i, BF16, tnorm)
            z = _mm(h, ssm_w_in_t, j, mode="plain", col_offs=(0,), n_cols=d_inner, tn=tn2, tm=tm, rc=rc_t,
                    seq_len=seq_len, transposed=True, name="ssm_in_z")
            xbc = _mm(h, ssm_w_in_t, j, mode="conv_silu", col_offs=(d_inner,), n_cols=d_inner + 2 * gn, tn=tn2,
                      tm=tm, rc=rc, seq_len=seq_len, transposed=True, conv_w=ssm_w_conv, conv_b=ssm_b_conv,
                      name="ssm_in_xbc")
            dt = _mm(h, ssm_w_in_t, j, mode="mask_cols", col_offs=(2 * d_inner + 2 * gn,), n_cols=n_heads,
                     tn=LANES, tm=tm, rc=rc_t, seq_len=seq_len, transposed=True, name="ssm_in_dt")
            yn = _ssd(xbc, z, dt, dt_bias, a_log, d_skip, ssm_norm_g, j, batch, seq_len, tl, SSD_GROUPS_PER_STEP)
            xf = _mm(yn, ssm_w_out, j, mode="res", col_offs=(0,), n_cols=d, tn=tn2, tm=tm_s, rc=tm_s,
                     seq_len=seq_len, res=xf, name="ssm_out")
        act = _mm(xf, ffn_w_up, i, mode="ffn", col_offs=(0, f), n_cols=f, tn=tn, tm=tm, rc=rc, seq_len=seq_len,
                  out_dtype=BF16, gain=ffn_gain, gain_layer=i, conv_w=ffn_w_dw, conv_b=ffn_b_dw,
                  name="ffn_up_conv_gate")
        xf = _mm(act, ffn_w_down, i, mode="res", col_offs=(0,), n_cols=d, tn=tn, tm=tm_k, rc=tm_k, seq_len=seq_len,
                 res=xf, name="ffn_down")
    out = _rmsnorm(xf, norm_final_g, 0, F32, tnorm)
    return out.reshape(batch, seq_len, d)
```

```python
import functools

import jax
import jax.numpy as jnp
from jax import lax
from jax.experimental import pallas as pl
from jax.experimental.pallas import tpu as pltpu

RMS_EPS = 1e-6
LN_EPS = 1e-5

SSD_CHUNK = 64
SSD_HEAD_DIM = 64
SSD_N_GROUPS = 8
SSD_D_STATE = 128
SSD_GROUPS_PER_STEP = 4
Q_SHIFT = SSD_CHUNK.bit_length() - 1
HEAD_SHIFT = SSD_HEAD_DIM.bit_length() - 1

LANES = 128
SUBLANES = 8
MXU_DIM = 256
VMEM_LIMIT_BYTES = 56 * 1024 * 1024

CONV_HALO = 32
NEG_BIG = -1e30
LOG2_E = 1.4426950408889634

F32 = jnp.float32
BF16 = jnp.bfloat16


def _params(semantics):
    return pltpu.CompilerParams(dimension_semantics=semantics, vmem_limit_bytes=VMEM_LIMIT_BYTES)


def _rmsnorm_kernel(x_ref, g_ref, o_ref):
    x = x_ref[...]
    ms = jnp.mean(x * x, axis=-1, keepdims=True)
    o_ref[...] = (x * lax.rsqrt(ms + RMS_EPS) * g_ref[...]).astype(o_ref.dtype)


def _rmsnorm(x, gains, layer, out_dtype, tm):
    t, d = x.shape
    return pl.pallas_call(
        _rmsnorm_kernel,
        out_shape=jax.ShapeDtypeStruct((t, d), out_dtype),
        grid=(t // tm,),
        in_specs=[pl.BlockSpec((tm, d), lambda m: (m, 0)),
                  pl.BlockSpec((None, 1, d), lambda m: (layer, 0, 0))],
        out_specs=pl.BlockSpec((tm, d), lambda m: (m, 0)),
        compiler_params=_params(("parallel",)),
        name="rmsnorm",
    )(x, gains)


def _tile_rows(x8, rows):
    return jnp.tile(x8, (rows // SUBLANES, 1))


def _causal_conv_rows(u, tail, p8_ref, kc):
    rc = u.shape[0]
    ext = jnp.concatenate([tail, u], axis=0)
    out = u * _tile_rows(p8_ref[kc - 1], rc) + _tile_rows(p8_ref[kc], rc)
    for k in range(kc - 1):
        start = SUBLANES - (kc - 1 - k)
        out = out + ext[start:start + rc, :] * _tile_rows(p8_ref[k], rc)
    return out


MODE_INPUTS = {"glu": 2, "res": 1, "res_bias": 2, "plain": 0, "mask_cols": 0, "conv_silu": 2, "ffn": 4}


def _mm_kernel(*refs, mode, nw, tiles_per_seq, n_valid, rc, transposed, norm):
    a_ref = refs[0]
    refs = refs[1:]
    if norm:
        gain_ref = refs[0]
        refs = refs[1:]
    w_refs = refs[:nw]
    rest = list(refs[nw:])
    if norm:
        rstd_ref = rest.pop(MODE_INPUTS[mode] + 1)
    m = pl.program_id(1)
    tm = a_ref.shape[0]
    conv_refs, p8_refs, carry_ref = (), (), None

    if mode == "glu":
        ba_ref, bg_ref, o_ref, *wb_refs = rest
    elif mode == "res":
        res_ref, o_ref, *wb_refs = rest
    elif mode == "res_bias":
        res_ref, b_ref, o_ref, *wb_refs = rest
    elif mode == "plain" or mode == "mask_cols":
        o_ref, *wb_refs = rest
    elif mode == "conv_silu":
        cw_ref, cb_ref, o_ref, wb0, p8, carry_ref = rest
        wb_refs, conv_refs, p8_refs = [wb0], [(cw_ref, cb_ref)], [p8]
    elif mode == "ffn":
        cwg_ref, cwv_ref, cbg_ref, cbv_ref, o_ref, wb0, wb1, p8g, p8v, carry_ref = rest
        wb_refs, conv_refs, p8_refs = [wb0, wb1], [(cwg_ref, cbg_ref), (cwv_ref, cbv_ref)], [p8g, p8v]
    else:
        raise ValueError(mode)
    kc = conv_refs[0][0].shape[0] if conv_refs else 0

    @pl.when(m == 0)
    def _():
        for w_ref, wb_ref in zip(w_refs, wb_refs):
            w = w_ref[...] * gain_ref[...] if norm else w_ref[...]
            wb_ref[...] = (w.T if transposed else w).astype(BF16)
        for (cw_ref, cb_ref), p8_ref in zip(conv_refs, p8_refs):
            for k in range(kc):
                p8_ref[k] = jnp.broadcast_to(cw_ref[k:k + 1, :], p8_ref.shape[1:])
            p8_ref[kc] = jnp.broadcast_to(cb_ref[...], p8_ref.shape[1:])

    if conv_refs:
        @pl.when((m % tiles_per_seq) == 0)
        def _():
            carry_ref[...] = jnp.zeros(carry_ref.shape, F32)

    if norm:
        @pl.when(pl.program_id(0) == 0)
        def _():
            x = a_ref[...]
            ms = jnp.mean(x * x, axis=-1, keepdims=True)
            rstd_ref[m] = jnp.broadcast_to(lax.rsqrt(ms + RMS_EPS), rstd_ref.shape[1:])

    def chunk_matmuls(c):
        rows = slice(c * rc, (c + 1) * rc)
        a = a_ref[rows, :].astype(BF16)
        accs = [jnp.dot(a, wb_ref[...], preferred_element_type=F32) for wb_ref in wb_refs]
        if norm:
            scale = jnp.tile(rstd_ref[m, rows, :], (1, accs[0].shape[1] // LANES))
            accs = [acc * scale for acc in accs]
        return accs

    tails = [carry_ref[i] for i in range(len(conv_refs))]
    n_chunks = tm // rc
    accs_next = chunk_matmuls(0)
    for c in range(n_chunks):
        rows = slice(c * rc, (c + 1) * rc)
        accs = accs_next
        if c + 1 < n_chunks:
            accs_next = chunk_matmuls(c + 1)
        if mode == "glu":
            o_ref[rows, :] = (accs[0] + ba_ref[...]) * jax.nn.sigmoid(accs[1] + bg_ref[...])
        elif mode == "res":
            o_ref[rows, :] = res_ref[rows, :] + accs[0]
        elif mode == "res_bias":
            o_ref[rows, :] = res_ref[rows, :] + (accs[0] + b_ref[...])
        elif mode == "plain":
            o_ref[rows, :] = accs[0].astype(o_ref.dtype)
        elif mode == "mask_cols":
            lane = lax.broadcasted_iota(jnp.int32, accs[0].shape, 1)
            o_ref[rows, :] = jnp.where(lane < n_valid, accs[0], 0.0)
        else:
            convs = [_causal_conv_rows(u, tail, p8_ref, kc) for u, tail, p8_ref in zip(accs, tails, p8_refs)]
            tails = [u[rc - SUBLANES:rc, :] for u in accs]
            if mode == "conv_silu":
                o_ref[rows, :] = convs[0] * jax.nn.sigmoid(convs[0])
            else:
                o_ref[rows, :] = (convs[0] * jax.nn.sigmoid(convs[0]) * convs[1]).astype(o_ref.dtype)
    for i, tail in enumerate(tails):
        carry_ref[i] = tail


def _mm(a, w, layer, *, mode, col_offs, n_cols, tn, tm, rc, seq_len, out_dtype=F32, transposed=False,
        gain=None, gain_layer=None, bias=None, conv_w=None, conv_b=None, res=None, name):
    t, k = a.shape
    nw = len(col_offs)
    norm = gain is not None
    assert not (norm and transposed)
    assert t % tm == 0 and seq_len % tm == 0 and tm % rc == 0 and all(off % tn == 0 for off in col_offs)
    n_tiles = pl.cdiv(n_cols, tn)
    out_cols = n_tiles * tn
    offs = [off // tn for off in col_offs]

    def col_spec(rows, ob):
        return pl.BlockSpec((None, rows, tn), lambda n, m: (layer, 0, n + ob))

    def weight_spec(ob):
        if transposed:
            return pl.BlockSpec((None, tn, k), lambda n, m: (layer, n + ob, 0))
        return col_spec(k, ob)

    in_specs = [pl.BlockSpec((tm, k), lambda n, m: (m, 0))]
    args = [a]
    if norm:
        in_specs.append(pl.BlockSpec((None, k, 1), lambda n, m: (gain_layer, 0, 0)))
        args.append(gain)
    for ob in offs:
        in_specs.append(weight_spec(ob))
        args.append(w)
    tile_spec = pl.BlockSpec((tm, tn), lambda n, m: (m, n))
    scratch = [pltpu.VMEM((k, tn), BF16) for _ in offs]

    if mode == "glu":
        in_specs += [col_spec(1, ob) for ob in offs]
        args += [bias, bias]
    elif mode == "res":
        in_specs.append(tile_spec)
        args.append(res)
    elif mode == "res_bias":
        in_specs += [tile_spec, col_spec(1, offs[0])]
        args += [res, bias]
    elif mode == "conv_silu":
        kc = conv_w.shape[1]
        in_specs += [col_spec(kc, 0), col_spec(1, 0)]
        args += [conv_w, conv_b]
    elif mode == "ffn":
        kc = conv_w.shape[1]
        in_specs += [col_spec(kc, ob) for ob in offs] + [col_spec(1, ob) for ob in offs]
        args += [conv_w, conv_w, conv_b, conv_b]
    if mode in ("conv_silu", "ffn"):
        assert kc - 1 <= SUBLANES
        scratch += [pltpu.VMEM((kc + 1, SUBLANES, tn), F32) for _ in offs]
        scratch += [pltpu.VMEM((nw, SUBLANES, tn), F32)]
    if norm:
        scratch = [pltpu.VMEM((t // tm, tm, LANES), F32)] + scratch

    kern = functools.partial(_mm_kernel, mode=mode, nw=nw, tiles_per_seq=seq_len // tm, n_valid=n_cols,
                             rc=rc, transposed=transposed, norm=norm)
    return pl.pallas_call(
        kern,
        out_shape=jax.ShapeDtypeStruct((t, out_cols), out_dtype),
        grid=(n_tiles, t // tm),
        in_specs=in_specs,
        out_specs=tile_spec,
        scratch_shapes=scratch,
        compiler_params=_params(("arbitrary" if norm else "parallel", "arbitrary")),
        name=name,
    )(*args)


def _conf_mid_kernel(cur_ref, halo_ref, w_ref, b_ref, g_ref, beta_ref, o_ref, ext_ref, conv_ref, w8_ref,
                     *, tm, tiles_per_seq, rc):
    kc = w_ref.shape[0]
    d = cur_ref.shape[1]
    seq_start = (pl.program_id(0) % tiles_per_seq) == 0

    @pl.when(seq_start)
    def _():
        ext_ref[0:CONV_HALO, :] = jnp.zeros((CONV_HALO, d), F32)

    @pl.when(jnp.logical_not(seq_start))
    def _():
        ext_ref[0:CONV_HALO, :] = halo_ref[...]

    ext_ref[CONV_HALO:CONV_HALO + tm, :] = cur_ref[...]
    for k in range(kc):
        w8_ref[k] = jnp.broadcast_to(w_ref[k:k + 1, :], (SUBLANES, d))

    base = CONV_HALO - (kc - 1)
    for cj in range(d // LANES):
        cols = slice(cj * LANES, (cj + 1) * LANES)

        def row_body(ri, carry, cols=cols):
            r0 = pl.multiple_of(ri * rc, rc)
            e = ext_ref[pl.ds(r0, rc + CONV_HALO), cols]
            acc = jnp.zeros((rc, LANES), F32) + b_ref[:, cols]
            for r in range(SUBLANES):
                taps = [k for k in range(kc) if (base + k) % SUBLANES == r]
                if not taps:
                    continue
                n = rc if r == 0 else rc + SUBLANES
                part = None
                for k in taps:
                    lo = (base + k) // SUBLANES * SUBLANES
                    term = e[lo:lo + n, :] * _tile_rows(w8_ref[k, :, cols], n)
                    part = term if part is None else part + term
                acc = acc + part[r:r + rc, :]
            conv_ref[pl.ds(r0, rc), cols] = acc
            return carry

        lax.fori_loop(0, tm // rc, row_body, 0)

    c = conv_ref[...]
    mu = jnp.mean(c, axis=-1, keepdims=True)
    xc = c - mu
    var = jnp.mean(xc * xc, axis=-1, keepdims=True)
    y = xc * lax.rsqrt(var + LN_EPS) * g_ref[...] + beta_ref[...]
    o_ref[...] = (y * jax.nn.sigmoid(y)).astype(o_ref.dtype)


def _conf_mid(v, w_dw, b_dw, ln_g, ln_b, layer, seq_len, tm):
    t, d = v.shape
    kc = w_dw.shape[1]
    assert kc - 1 <= CONV_HALO and seq_len % tm == 0 and tm % CONV_HALO == 0 and d % LANES == 0
    rc = min(128, tm)
    halo_per_tile = tm // CONV_HALO
    row = lambda m: (layer, 0, 0)
    kern = functools.partial(_conf_mid_kernel, tm=tm, tiles_per_seq=seq_len // tm, rc=rc)
    return pl.pallas_call(
        kern,
        out_shape=jax.ShapeDtypeStruct((t, d), BF16),
        grid=(t // tm,),
        in_specs=[pl.BlockSpec((tm, d), lambda m: (m, 0)),
                  pl.BlockSpec((CONV_HALO, d), lambda m: (jnp.maximum(m * halo_per_tile - 1, 0), 0)),
                  pl.BlockSpec((None, kc, d), row),
                  pl.BlockSpec((None, 1, d), row),
                  pl.BlockSpec((None, 1, d), row),
                  pl.BlockSpec((None, 1, d), row)],
        out_specs=pl.BlockSpec((tm, d), lambda m: (m, 0)),
        scratch_shapes=[pltpu.VMEM((CONV_HALO + tm, d), F32), pltpu.VMEM((tm, d), F32),
                        pltpu.VMEM((kc, SUBLANES, d), F32)],
        compiler_params=_params(("parallel",)),
        name="conformer_conv_ln_swish",
    )(v, v, w_dw, b_dw, ln_g, ln_b)


def _split2(x):
    hi = x.astype(BF16)
    lo = (x - hi.astype(F32)).astype(BF16)
    return hi, lo


def _split3(x):
    hi, lo = _split2(x)
    lo2 = (x - hi.astype(F32) - lo.astype(F32)).astype(BF16)
    return hi, lo, lo2


def _ssd_kernel(x_ref, b_ref, c_ref, z_ref, dt_ref, dtb_ref, alog_ref, dskip_ref, ng_ref, o_ref,
                state_ref, parts_ref, dte_ref, acse_ref, *, tl, gw, gs):
    q = SSD_CHUNK
    lt = pl.program_id(1)
    g0 = pl.program_id(2) * gs
    hpg = gw // SSD_HEAD_DIM

    @pl.when(lt == 0)
    def _():
        for gi in range(gs):
            state_ref[g0 + gi] = jnp.zeros(state_ref.shape[1:], F32)

    @pl.when(g0 == 0)
    def _():
        pre = dt_ref[...] + dtb_ref[...]
        dt = jnp.maximum(pre, 0.0) + jnp.log1p(jnp.exp(-jnp.abs(pre)))
        a = dt * (-jnp.exp(alog_ref[...]) * LOG2_E)
        r = lax.broadcasted_iota(jnp.int32, (tl, tl), 0)
        c = lax.broadcasted_iota(jnp.int32, (tl, tl), 1)
        tri = jnp.where((r >= c) & ((r >> Q_SHIFT) == (c >> Q_SHIFT)), 1.0, 0.0).astype(BF16)
        acs = jnp.zeros((tl, LANES), F32)
        for part in _split3(a):
            acs = acs + jnp.dot(tri, part, preferred_element_type=F32)
        for i, part in enumerate(_split2(dt) + _split2(acs)):
            parts_ref[i * tl:(i + 1) * tl, :] = part

    hrow = lax.broadcasted_iota(jnp.int32, (LANES, gs * gw), 0)
    hcol = lax.broadcasted_iota(jnp.int32, (LANES, gs * gw), 1)
    expand = jnp.where(hrow == g0 * hpg + (hcol >> HEAD_SHIFT), 1.0, 0.0).astype(BF16)
    ex = jnp.dot(parts_ref[...], expand, preferred_element_type=F32)
    dte_ref[...] = ex[0:tl] + ex[tl:2 * tl]
    acse_ref[...] = ex[2 * tl:3 * tl] + ex[3 * tl:4 * tl]

    row = lax.broadcasted_iota(jnp.int32, (q, gw), 0)
    pos = lax.broadcasted_iota(jnp.int32, (q, gw), 1) & (SSD_HEAD_DIM - 1)
    causal = row >= pos
    diagonal = row == pos
    br = lax.broadcasted_iota(jnp.int32, (MXU_DIM, MXU_DIM), 0) >> HEAD_SHIFT
    bc = lax.broadcasted_iota(jnp.int32, (MXU_DIM, MXU_DIM), 1) >> HEAD_SHIFT
    head_mask = jnp.where(br == bc, 1.0, 0.0).astype(BF16)
    heads_per_dot = MXU_DIM // SSD_HEAD_DIM

    for ci in range(tl // q):
        rows = pl.ds(ci * q, q)
        for gi in range(gs):
            lanes = slice(gi * gw, (gi + 1) * gw)
            nlanes = slice(gi * SSD_D_STATE, (gi + 1) * SSD_D_STATE)
            x = x_ref[rows, lanes]
            bm = b_ref[rows, nlanes].astype(BF16)
            cm = c_ref[rows, nlanes].astype(BF16)
            acs = acse_ref[rows, lanes]
            a_last = acse_ref[pl.ds(ci * q + q - 1, 1), lanes]
            xd = x * dte_ref[rows, lanes]
            xd_bf = xd.astype(BF16)

            prev = state_ref[g0 + gi]
            y = jnp.dot(cm, prev.astype(BF16), preferred_element_type=F32) * jnp.exp2(acs)

            wgt = (jnp.exp2(a_last - acs) * xd).astype(BF16)
            s_new = lax.dot_general(bm, wgt, (((0,), (0,)), ((), ())), preferred_element_type=F32)
            state_ref[g0 + gi] = prev * jnp.exp2(a_last) + s_new

            cb = lax.dot_general(cm, jnp.concatenate([bm] * hpg, axis=0), (((1,), (1,)), ((), ())),
                                 preferred_element_type=F32)
            acs_s = jnp.sum(jnp.where(diagonal, acs, 0.0), axis=0, keepdims=True)
            decay = jnp.exp2(jnp.where(causal, acs - acs_s, NEG_BIG))
            mix = (cb * decay).astype(BF16)
            parts = []
            for j in range(gw // MXU_DIM):
                cols = slice(j * MXU_DIM, (j + 1) * MXU_DIM)
                blockdiag = jnp.concatenate([xd_bf[:, cols]] * heads_per_dot, axis=0) * head_mask
                parts.append(jnp.dot(mix[:, cols], blockdiag, preferred_element_type=F32))
            y = y + jnp.concatenate(parts, axis=1) + dskip_ref[:, lanes] * x

            z = z_ref[rows, lanes]
            yg = y * (z * jax.nn.sigmoid(z))
            ms = jnp.mean(yg * yg, axis=-1, keepdims=True)
            o_ref[rows, lanes] = (yg * lax.rsqrt(ms + RMS_EPS) * ng_ref[:, lanes]).astype(o_ref.dtype)


def _ssd(xbc, z, dt, dt_bias, a_log, d_skip, norm_g, layer, batch, seq_len, tl, gs):
    t, d_inner = z.shape
    gw = d_inner // SSD_N_GROUPS
    bw = gs * SSD_D_STATE
    assert gw % MXU_DIM == 0 and SSD_D_STATE == LANES and seq_len % tl == 0 and tl % SSD_CHUNK == 0
    assert SSD_N_GROUPS % gs == 0 and d_inner % bw == 0
    nl = seq_len // tl
    b_blk = d_inner // bw
    c_blk = b_blk + SSD_N_GROUPS // gs
    rows = lambda b, l, g: (b * nl + l, g)
    head_row = lambda b, l, g: (layer, 0, 0)
    group_row = lambda b, l, g: (layer, 0, g)
    kern = functools.partial(_ssd_kernel, tl=tl, gw=gw, gs=gs)
    return pl.pallas_call(
        kern,
        out_shape=jax.ShapeDtypeStruct((t, d_inner), BF16),
        grid=(batch, nl, SSD_N_GROUPS // gs),
        in_specs=[pl.BlockSpec((tl, gs * gw), rows),
                  pl.BlockSpec((tl, bw), lambda b, l, g: (b * nl + l, b_blk + g)),
                  pl.BlockSpec((tl, bw), lambda b, l, g: (b * nl + l, c_blk + g)),
                  pl.BlockSpec((tl, gs * gw), rows),
                  pl.BlockSpec((tl, LANES), lambda b, l, g: (b * nl + l, 0)),
                  pl.BlockSpec((None, 1, LANES), head_row),
                  pl.BlockSpec((None, 1, LANES), head_row),
                  pl.BlockSpec((None, 1, gs * gw), group_row),
                  pl.BlockSpec((None, 1, gs * gw), group_row)],
        out_specs=pl.BlockSpec((tl, gs * gw), rows),
        scratch_shapes=[pltpu.VMEM((SSD_N_GROUPS, SSD_D_STATE, gw), F32),
                        pltpu.VMEM((4 * tl, LANES), BF16),
                        pltpu.VMEM((tl, gs * gw), F32),
                        pltpu.VMEM((tl, gs * gw), F32)],
        compiler_params=_params(("parallel", "arbitrary", "arbitrary")),
        name="ssd_scan_gate_norm",
    )(xbc, xbc, xbc, z, dt, dt_bias, a_log, d_skip, norm_g)


def _row3(p):
    return p.reshape(p.shape[0], 1, p.shape[1])


def _pad_lanes(p):
    return jnp.pad(p, ((0, 0), (0, LANES - p.shape[1])))


def kernel(x, norm_mix_g, norm_ffn_g, norm_final_g, cv_w_in, cv_b_in, cv_w_dw, cv_b_dw, cv_ln_g, cv_ln_b, cv_w_out, cv_b_out, ssm_w_in, ssm_w_conv, ssm_b_conv, ssm_dt_bias, ssm_a_log, ssm_d, ssm_norm_g, ssm_w_out, ffn_w_up, ffn_w_dw, ffn_b_dw, ffn_w_down):
    batch, seq_len, d = x.shape
    t = batch * seq_len
    depth = norm_mix_g.shape[0]
    d_inner = ssm_w_out.shape[1]
    n_heads = ssm_dt_bias.shape[1]
    gn = SSD_N_GROUPS * SSD_D_STATE
    f = ffn_w_down.shape[1]
    assert n_heads * SSD_HEAD_DIM == d_inner and n_heads <= LANES
    assert ssm_w_in.shape[2] == 2 * d_inner + 2 * gn + n_heads

    tm = min(1024, seq_len)
    tm_k = min(512, seq_len)
    tm_s = min(256, seq_len)
    tn = min(512, d)
    tn2 = min(1024, d)
    tnorm = min(1024, seq_len)
    tconv = min(512, seq_len)
    tl = min(512, seq_len)
    rc = min(256, seq_len)
    rc_t = min(512, seq_len)

    mix_gain = norm_mix_g.reshape(depth, d, 1)
    ffn_gain = norm_ffn_g.reshape(depth, d, 1)
    norm_mix_g = _row3(norm_mix_g)
    norm_final_g = norm_final_g.reshape(1, 1, d)
    cv_b_in, cv_b_dw, cv_ln_g, cv_ln_b, cv_b_out = map(_row3, (cv_b_in, cv_b_dw, cv_ln_g, cv_ln_b, cv_b_out))
    ssm_b_conv, ssm_norm_g, ffn_b_dw = map(_row3, (ssm_b_conv, ssm_norm_g, ffn_b_dw))
    dt_bias = _row3(_pad_lanes(ssm_dt_bias))
    a_log = _row3(_pad_lanes(ssm_a_log))
    d_skip = _row3(jnp.repeat(ssm_d, SSD_HEAD_DIM, axis=1))
    ssm_w_in_t = jnp.swapaxes(ssm_w_in, 1, 2)

    xf = x.reshape(t, d)
    for i in range(depth):
        j = i // 2
        if i % 2 == 0:
            v = _mm(xf, cv_w_in, j, mode="glu", col_offs=(0, d), n_cols=d, tn=tn, tm=tm, rc=rc, seq_len=seq_len,
                    gain=mix_gain, gain_layer=i, bias=cv_b_in, name="conformer_in_glu")
            s = _conf_mid(v, cv_w_dw, cv_b_dw, cv_ln_g, cv_ln_b, j, seq_len, tconv)
            xf = _mm(s, cv_w_out, j, mode="res_bias", col_offs=(0,), n_cols=d, tn=tn2, tm=tm, rc=rc,
                     seq_len=seq_len, res=xf, bias=cv_b_out, name="conformer_out")
        else:
            h = _rmsnorm(xf, norm_mix_g, i, BF16, tnorm)
            z = _mm(h, ssm_w_in_t, j, mode="plain", col_offs=(0,), n_cols=d_inner, tn=tn2, tm=tm, rc=rc_t,
                    seq_len=seq_len, transposed=True, name="ssm_in_z")
            xbc = _mm(h, ssm_w_in_t, j, mode="conv_silu", col_offs=(d_inner,), n_cols=d_inner + 2 * gn, tn=tn2,
                      tm=tm, rc=rc, seq_len=seq_len, transposed=True, conv_w=ssm_w_conv, conv_b=ssm_b_conv,
                      name="ssm_in_xbc")
            dt = _mm(h, ssm_w_in_t, j, mode="mask_cols", col_offs=(2 * d_inner + 2 * gn,), n_cols=n_heads,
                     tn=LANES, tm=tm, rc=rc_t, seq_len=seq_len, transposed=True, name="ssm_in_dt")
            yn = _ssd(xbc, z, dt, dt_bias, a_log, d_skip, ssm_norm_g, j, batch, seq_len, tl, SSD_GROUPS_PER_STEP)
            xf = _mm(yn, ssm_w_out, j, mode="res", col_offs=(0,), n_cols=d, tn=tn2, tm=tm_s, rc=tm_s,
                     seq_len=seq_len, res=xf, name="ssm_out")
        act = _mm(xf, ffn_w_up, i, mode="ffn", col_offs=(0, f), n_cols=f, tn=tn, tm=tm, rc=rc, seq_len=seq_len,
                  out_dtype=BF16, gain=ffn_gain, gain_layer=i, conv_w=ffn_w_dw, conv_b=ffn_b_dw,
                  name="ffn_up_conv_gate")
        xf = _mm(act, ffn_w_down, i, mode="res", col_offs=(0,), n_cols=d, tn=tn, tm=tm_k, rc=tm_k, seq_len=seq_len,
                 res=xf, name="ffn_down")
    out = _rmsnorm(xf, norm_final_g, 0, F32, tnorm)
    return out.reshape(batch, seq_len, d)
```
